```python
import jax, jax.numpy as jnp
from jax import lax
import numpy as np

D_MODEL = 1024
BATCH = 4
SEQ = 4096
DEPTH = 4
DEC_BATCH = 32
DEC_SEQ = 1
PAST_LEN = 8192
PAGE_SIZE = 128

N_GROUPS = 4
GROUP_WIDTH = D_MODEL // N_GROUPS
HEAD_DIM = 64
N_HEADS = GROUP_WIDTH // HEAD_DIM
N_SPLITS = 11
D_IN = N_SPLITS * GROUP_WIDTH
CHUNK = 128
Q_BLOCK = 128
CONV_WIDTH = 31
D_FF = 4 * D_MODEL
EPS = 1e-6
ROPE_BASE = 10000.0
SB_BIAS_LO = -8.0
SB_BIAS_HI = -5.0

kernel_name = "hybrid_sgu_stickbreak_retention_conformer_step"


def _standardize(x):
    xf = x.astype(jnp.float32)
    mu = jnp.mean(xf, axis=-1, keepdims=True)
    xc = xf - mu
    return xc * lax.rsqrt(jnp.mean(xc * xc, axis=-1, keepdims=True) + EPS)


def rmsnorm(x, g):
    xf = x.astype(jnp.float32)
    y = xf * lax.rsqrt(jnp.mean(xf * xf, axis=-1, keepdims=True) + EPS) * g.astype(jnp.float32)
    return y.astype(x.dtype)


def layernorm(x, g, b):
    return (_standardize(x) * g.astype(jnp.float32) + b.astype(jnp.float32)).astype(x.dtype)


def rope(x, pos):
    half = HEAD_DIM // 2
    inv = ROPE_BASE ** (-jnp.arange(0, HEAD_DIM, 2, dtype=jnp.float32) / HEAD_DIM)
    ang = pos.astype(jnp.float32)[:, None] * inv[None, :]
    cos = jnp.cos(ang)[None, :, None, :]
    sin = jnp.sin(ang)[None, :, None, :]
    xf = x.astype(jnp.float32)
    x1, x2 = xf[..., :half], xf[..., half:]
    return jnp.concatenate([x1 * cos - x2 * sin, x1 * sin + x2 * cos], axis=-1).astype(x.dtype)


def spatial_gating(u, v, w_s, b_s, ln_g, ln_b):
    B, L, _ = u.shape
    c = CHUNK if L % CHUNK == 0 else L
    vn = layernorm(v, ln_g, ln_b)
    vr = vn.reshape(B, L // c, c, N_HEADS, HEAD_DIM)
    ws = w_s[:, :c, :c] * jnp.tril(jnp.ones((c, c), w_s.dtype))
    mixed = jnp.einsum('gts,bnsgd->bntgd', ws, vr) + b_s[:, :c].T[:, :, None]
    return u * mixed.reshape(B, L, GROUP_WIDTH), vn


def stick_breaking_block(q, k, v, bias, q_pos, k_pos):
    z = jnp.einsum('bqhd,bkhd->bhqk', q, k).astype(jnp.float32) * (HEAD_DIM ** -0.5)
    z = z + bias.astype(jnp.float32)[None, :, None, None]
    causal = (k_pos[None, :] < q_pos[:, None])[None, None]
    log_stay = jnp.where(causal, jax.nn.log_sigmoid(-z), 0.0)
    later = lax.cumsum(log_stay, axis=3, reverse=True) - log_stay
    w = jnp.where(causal, jnp.exp(jax.nn.log_sigmoid(z) + later), 0.0)
    return jnp.einsum('bhqk,bkhd->bqhd', w.astype(v.dtype), v)


def stick_breaking_prompt(q, k, v, bias):
    B, L, H, D = q.shape
    nb = L // Q_BLOCK
    qb = q.reshape(B, nb, Q_BLOCK, H, D).transpose(1, 0, 2, 3, 4)
    qpos = jnp.arange(L).reshape(nb, Q_BLOCK)
    kpos = jnp.arange(L)
    out = lax.map(lambda a: stick_breaking_block(a[0], k, v, bias, a[1], kpos), (qb, qpos))
    return out.transpose(1, 0, 2, 3, 4).reshape(B, L, H, D)


def stick_breaking_sample(q, k_new, v_new, bias, k_pool, v_pool, page_table):
    Bd, Ls = q.shape[:2]
    past_len = page_table.shape[1] * k_pool.shape[1]
    k_past = k_pool[page_table].reshape(Bd, past_len, N_HEADS, HEAD_DIM)
    v_past = v_pool[page_table].reshape(Bd, past_len, N_HEADS, HEAD_DIM)
    k_all = jnp.concatenate([k_past.astype(k_new.dtype), k_new], axis=1)
    v_all = jnp.concatenate([v_past.astype(v_new.dtype), v_new], axis=1)
    q_pos = past_len + jnp.arange(Ls)
    k_pos = jnp.arange(past_len + Ls)
    return stick_breaking_block(q, k_all, v_all, bias, q_pos, k_pos)


def retention(q, k, v, s0):
    B, L, H, D = q.shape
    c = CHUNK if L % CHUNK == 0 else L
    n = L // c
    f32 = jnp.float32
    lg = jnp.log1p(-jnp.exp2(-5.0 - jnp.arange(H, dtype=f32)))
    idx = jnp.arange(c, dtype=f32)
    diff = idx[:, None] - idx[None, :]
    dmask = jnp.where(diff >= 0, jnp.exp(jnp.maximum(diff, 0.0) * lg[:, None, None]), 0.0)
    q_decay = jnp.exp((idx[None, :] + 1.0) * lg[:, None])
    k_decay = jnp.exp((c - 1.0 - idx[None, :]) * lg[:, None])
    chunk_decay = jnp.exp(c * lg)

    def to_chunks(t):
        return t.astype(f32).reshape(B, n, c, H, D).transpose(1, 0, 2, 3, 4)

    def step(s, inp):
        qc, kc, vc = inp
        scores = jnp.einsum('bihd,bjhd->bhij', qc, kc) * dmask
        intra = jnp.einsum('bhij,bjhe->bihe', scores, vc)
        inter = jnp.einsum('bihd,bhde->bihe', qc, s) * q_decay.T[None, :, :, None]
        s_new = s * chunk_decay[None, :, None, None] + jnp.einsum('bjhd,bjhe,hj->bhde', kc, vc, k_decay)
        return s_new, intra + inter

    s_fin, out = lax.scan(step, s0.astype(f32), (to_chunks(q), to_chunks(k), to_chunks(v)))
    return out.transpose(1, 0, 2, 3, 4).reshape(B, L, H, D), s_fin


def conformer_conv(a, gate, buf, w, b, ln_g, ln_b):
    glu = a * jax.nn.sigmoid(gate)
    xp = jnp.concatenate([buf.astype(glu.dtype), glu], axis=1)
    y = lax.conv_general_dilated(xp, w[:, None, :].astype(xp.dtype), window_strides=(1,), padding='VALID',
                                 dimension_numbers=('NWC', 'WIO', 'NWC'),
                                 feature_group_count=GROUP_WIDTH) + b.astype(xp.dtype)
    y = layernorm(y, ln_g, ln_b)
    return y * jax.nn.sigmoid(y), xp[:, -(CONV_WIDTH - 1):]


def token_mixers(h, pos, attend, ret0, conv0, w_in, w_out, sgu_w, sgu_b, sgu_ln_g, sgu_ln_b,
                 conv_w, conv_b, conv_ln_g, conv_ln_b, sb_bias):
    B, L, _ = h.shape
    u_a, v_a, q_b, k_b, v_b, q_c, k_c, v_c, g_c, a_d, gate_d = jnp.split(h @ w_in, N_SPLITS, axis=-1)

    def heads(t):
        return t.reshape(B, L, N_HEADS, HEAD_DIM)

    out_a, v_rows = spatial_gating(jax.nn.gelu(u_a), jax.nn.gelu(v_a), sgu_w, sgu_b, sgu_ln_g, sgu_ln_b)
    k_b, v_b = heads(k_b), heads(v_b)
    out_b = attend(heads(q_b), k_b, v_b, sb_bias).reshape(B, L, GROUP_WIDTH)
    qr = rope(heads(q_c), pos)
    kr = rope(heads(k_c), pos) * (HEAD_DIM ** -0.5)
    o_c, ret_new = retention(qr, kr, heads(v_c), ret0)
    out_c = jax.nn.silu(g_c) * _standardize(o_c).reshape(B, L, GROUP_WIDTH).astype(h.dtype)
    out_d, conv_new = conformer_conv(a_d, gate_d, conv0, conv_w, conv_b, conv_ln_g, conv_ln_b)

    y = jnp.concatenate([out_a, out_b, out_c, out_d], axis=-1) @ w_out
    return y, k_b, v_b, v_rows, ret_new, conv_new


def squared_relu_mlp(h, w1, w2):
    a = jax.nn.relu(h @ w1)
    return (a * a) @ w2


def setup_inputs(seed: int = 0) -> dict:
    key = jax.random.key(seed)
    ks = jax.random.split(key, 24)
    n_pages = PAST_LEN // PAGE_SIZE
    n_used = DEC_BATCH * n_pages
    n_pool = n_used + max(1, n_used // 4)
    page_table = jax.random.permutation(ks[0], n_pool)[:n_used].reshape(DEC_BATCH, n_pages).astype(jnp.int32)
    nrm = jax.random.normal
    f32 = jnp.float32
    sb_base = jnp.linspace(SB_BIAS_LO, SB_BIAS_HI, N_HEADS, dtype=f32)
    return {
        'x_prompt': nrm(ks[1], (BATCH, SEQ, D_MODEL), f32),
        'x_sample': nrm(ks[2], (DEC_BATCH, DEC_SEQ, D_MODEL), f32),
        'cache_k': nrm(ks[3], (DEPTH, n_pool, PAGE_SIZE, N_HEADS, HEAD_DIM), f32),
        'cache_v': nrm(ks[4], (DEPTH, n_pool, PAGE_SIZE, N_HEADS, HEAD_DIM), f32),
        'state_ret': 0.5 * nrm(ks[5], (DEPTH, DEC_BATCH, N_HEADS, HEAD_DIM, HEAD_DIM), f32),
        'state_conv': 0.5 * nrm(ks[6], (DEPTH, DEC_BATCH, CONV_WIDTH - 1, GROUP_WIDTH), f32),
        'page_table': page_table,
        'w_in': nrm(ks[7], (DEPTH, D_MODEL, D_IN), f32) * D_MODEL ** -0.5,
        'w_out': nrm(ks[8], (DEPTH, D_MODEL, D_MODEL), f32) * D_MODEL ** -0.5,
        'sgu_w': nrm(ks[9], (DEPTH, N_HEADS, CHUNK, CHUNK), f32) * CHUNK ** -0.5,
        'sgu_b': 1.0 + 0.1 * nrm(ks[10], (DEPTH, N_HEADS, CHUNK), f32),
        'sgu_ln_g': 1.0 + 0.01 * nrm(ks[11], (DEPTH, GROUP_WIDTH), f32),
        'sgu_ln_b': 0.01 * nrm(ks[12], (DEPTH, GROUP_WIDTH), f32),
        'conv_w': nrm(ks[13], (DEPTH, CONV_WIDTH, GROUP_WIDTH), f32) * CONV_WIDTH ** -0.5,
        'conv_b': 0.01 * nrm(ks[14], (DEPTH, GROUP_WIDTH), f32),
        'conv_ln_g': 1.0 + 0.01 * nrm(ks[15], (DEPTH, GROUP_WIDTH), f32),
        'conv_ln_b': 0.01 * nrm(ks[16], (DEPTH, GROUP_WIDTH), f32),
        'sb_bias': sb_base[None, :] + 0.1 * nrm(ks[22], (DEPTH, N_HEADS), f32),
        'norm_mix': 1.0 + 0.01 * nrm(ks[17], (DEPTH, D_MODEL), f32),
        'norm_ffn': 1.0 + 0.01 * nrm(ks[18], (DEPTH, D_MODEL), f32),
        'norm_final': 1.0 + 0.01 * nrm(ks[19], (D_MODEL,), f32),
        'w_ff1': nrm(ks[20], (DEPTH, D_MODEL, D_FF), f32) * D_MODEL ** -0.5,
        'w_ff2': nrm(ks[21], (DEPTH, D_FF, D_MODEL), f32) * D_FF ** -0.5,
    }


def reference(x_prompt, x_sample, cache_k, cache_v, state_ret, state_conv, page_table,
              w_in, w_out, sgu_w, sgu_b, sgu_ln_g, sgu_ln_b, conv_w, conv_b, conv_ln_g, conv_ln_b,
              sb_bias, norm_mix, norm_ffn, norm_final, w_ff1, w_ff2):
    Bp, Lp, _ = x_prompt.shape
    Bs, Ls, _ = x_sample.shape
    pos_p = jnp.arange(Lp)
    pos_s = PAST_LEN + jnp.arange(Ls)
    hp, hs = x_prompt, x_sample
    kp, vp, ks_, vs_, retp, rets, convp, convs, sguv = [], [], [], [], [], [], [], [], []
    for l in range(DEPTH):
        lw = (w_in[l], w_out[l], sgu_w[l], sgu_b[l], sgu_ln_g[l], sgu_ln_b[l],
              conv_w[l], conv_b[l], conv_ln_g[l], conv_ln_b[l], sb_bias[l])
        ret0_p = jnp.zeros((Bp, N_HEADS, HEAD_DIM, HEAD_DIM), jnp.float32)
        conv0_p = jnp.zeros((Bp, CONV_WIDTH - 1, GROUP_WIDTH), hp.dtype)
        yp, k1, v1, _, r1, c1 = token_mixers(rmsnorm(hp, norm_mix[l]), pos_p, stick_breaking_prompt,
                                             ret0_p, conv0_p, *lw)
        hp = hp + yp
        hp = hp + squared_relu_mlp(rmsnorm(hp, norm_ffn[l]), w_ff1[l], w_ff2[l])
        attend_s = lambda q, k, v, bias, l=l: stick_breaking_sample(q, k, v, bias, cache_k[l], cache_v[l],
                                                                    page_table)
        ys, k2, v2, vr2, r2, c2 = token_mixers(rmsnorm(hs, norm_mix[l]), pos_s, attend_s,
                                               state_ret[l], state_conv[l], *lw)
        hs = hs + ys
        hs = hs + squared_relu_mlp(rmsnorm(hs, norm_ffn[l]), w_ff1[l], w_ff2[l])
        kp.append(k1); vp.append(v1); retp.append(r1); convp.append(c1)
        ks_.append(k2); vs_.append(v2); rets.append(r2); convs.append(c2); sguv.append(vr2)
    y_prompt = rmsnorm(hp, norm_final)
    y_sample = rmsnorm(hs, norm_final)
    return (y_prompt, y_sample, jnp.stack(kp), jnp.stack(vp), jnp.stack(ks_), jnp.stack(vs_),
            jnp.stack(retp), jnp.stack(rets), jnp.stack(convp), jnp.stack(convs), jnp.stack(sguv))
```

```python
import functools

import jax
import jax.numpy as jnp
from jax import lax
from jax.experimental import pallas as pl
from jax.experimental.pallas import tpu as pltpu

F32 = jnp.float32
BF16 = jnp.bfloat16

D_MODEL = 1024
GROUP_WIDTH = 256
N_HEADS = 4
HEAD_DIM = 64
N_SPLITS = 11
CHUNK = 128
CONV_WIDTH = 31
CONV_TAIL = 32
D_FF = 4 * D_MODEL
FF_CHUNK = 1024
EPS = 1e-6
ROPE_BASE = 10000.0
ATTN_TQ = 128
ATTN_TK = 256
SOFTPLUS_LINEAR_ABOVE = 30.0
VMEM_LIMIT_BYTES = 56 * 1024 * 1024


def _cparams(n_axes):
    return pltpu.CompilerParams(dimension_semantics=("arbitrary",) * n_axes,
                                vmem_limit_bytes=VMEM_LIMIT_BYTES)


def _resident(shape):
    nd = len(shape)
    return pl.BlockSpec(shape, lambda *_: (0,) * nd, pipeline_mode=pl.Buffered(1))


def _gelu_tanh(x):
    return x * (0.5 * (1.0 + jnp.tanh(0.7978845608028654 * (x + 0.044715 * (x * x * x)))))


def _sigmoid(x):
    return 1.0 / (1.0 + jnp.exp(-x))


def _standardize(x):
    mu = jnp.mean(x, axis=-1, keepdims=True)
    xc = x - mu
    return xc * lax.rsqrt(jnp.mean(xc * xc, axis=-1, keepdims=True) + EPS)


def _rms_scale(x, g):
    return x * lax.rsqrt(jnp.mean(x * x, axis=-1, keepdims=True) + EPS) * g


def _softplus(z):
    return jnp.where(z > SOFTPLUS_LINEAR_ABOVE, z,
                     jnp.log(1.0 + jnp.exp(jnp.minimum(z, SOFTPLUS_LINEAR_ABOVE))))


def _head_masks(shape):
    lane = lax.broadcasted_iota(jnp.int32, shape, len(shape) - 1)
    return [(lane >= h * HEAD_DIM) & (lane < (h + 1) * HEAD_DIM) for h in range(N_HEADS)]


def _stack_heads(x, masks):
    zero = jnp.zeros_like(x)
    return jnp.concatenate([jnp.where(m, x, zero) for m in masks], axis=0)


def _split_bf16(x):
    hi = x.astype(BF16)
    lo = (x - hi.astype(F32)).astype(BF16)
    return hi, lo


def _dot(a, b):
    return jnp.dot(a, b, preferred_element_type=F32)


def _dot_nt(a, b):
    return lax.dot_general(a, b, (((1,), (1,)), ((), ())), preferred_element_type=F32)


def _dot_tn(a, b):
    return lax.dot_general(a, b, (((0,), (0,)), ((), ())), preferred_element_type=F32)


def _inproj_kernel(x_ref, g_ref, w_ref, cos_ref, s1_ref, s2_ref, lng_ref, lnb_ref,
                   ua_ref, vn_ref, q_ref, k_ref, v_ref, qr_ref, kr_ref, vc_ref, sg_ref, glu_ref):
    xb = _rms_scale(x_ref[...], g_ref[...]).astype(BF16)

    def proj(i):
        return _dot(xb, w_ref[:, i * GROUP_WIDTH:(i + 1) * GROUP_WIDTH])

    cos, s1, s2 = cos_ref[...], s1_ref[...], s2_ref[...]

    def rope(t):
        return (t * cos + pltpu.roll(t, GROUP_WIDTH - HEAD_DIM // 2, 1) * s1
                + pltpu.roll(t, HEAD_DIM // 2, 1) * s2)

    ua_ref[...] = _gelu_tanh(proj(0)).astype(ua_ref.dtype)
    vn = _standardize(_gelu_tanh(proj(1))) * lng_ref[...] + lnb_ref[...]
    vn_ref[...] = vn.astype(vn_ref.dtype)
    q_ref[...] = (proj(2) * (HEAD_DIM ** -0.5)).astype(q_ref.dtype)
    k_ref[...] = proj(3)
    v_ref[...] = proj(4)
    qr_ref[...] = rope(proj(5)).astype(qr_ref.dtype)
    kr_ref[...] = (rope(proj(6)) * (HEAD_DIM ** -0.5)).astype(kr_ref.dtype)
    vc_ref[...] = proj(7).astype(vc_ref.dtype)
    g = proj(8)
    sg_ref[...] = (g * _sigmoid(g)).astype(sg_ref.dtype)
    glu_ref[...] = proj(9) * _sigmoid(proj(10))


def _inproj(x, g, w_bf16, cos, s1, s2, lng, lnb, *, tm, mm_dtype):
    rows = x.shape[0]
    n_pos_blocks = cos.shape[0] // tm
    row_blk = lambda i: (i, 0)
    pos_blk = lambda i: (i % n_pos_blocks, 0)
    gw = pl.BlockSpec((tm, GROUP_WIDTH), row_blk)
    out_dtypes = [F32, mm_dtype, mm_dtype, F32, F32, mm_dtype, mm_dtype, mm_dtype, F32, F32]
    return pl.pallas_call(
        _inproj_kernel,
        grid=(rows // tm,),
        in_specs=[pl.BlockSpec((tm, D_MODEL), row_blk),
                  _resident((1, D_MODEL)),
                  _resident((D_MODEL, N_SPLITS * GROUP_WIDTH)),
                  pl.BlockSpec((tm, GROUP_WIDTH), pos_blk),
                  pl.BlockSpec((tm, GROUP_WIDTH), pos_blk),
                  pl.BlockSpec((tm, GROUP_WIDTH), pos_blk),
                  _resident((1, GROUP_WIDTH)),
                  _resident((1, GROUP_WIDTH))],
        out_specs=[gw] * len(out_dtypes),
        out_shape=[jax.ShapeDtypeStruct((rows, GROUP_WIDTH), dt) for dt in out_dtypes],
        compiler_params=_cparams(1),
        name="inproj",
    )(x, g, w_bf16, cos, s1, s2, lng, lnb)


def _mix_kernel(ua_ref, vn_ref, qr_ref, kr_ref, vc_ref, sg_ref, glu_ref,
                wsg_ref, bsg_ref, dmask_ref, qdec_ref, kdec_ref, cdec_ref,
                cw_ref, cb_ref, clg_ref, clb_ref,
                oa_ref, oc_ref, od_ref, ret_ref, cst_ref,
                s_scr, buf_scr, *, tc, conv_rows):
    j = pl.program_id(1)

    @pl.when(j == 0)
    def _():
        s_scr[...] = jnp.zeros(s_scr.shape, F32)
        buf_scr[0:CONV_TAIL, :] = jnp.zeros((CONV_TAIL, GROUP_WIDTH), F32)

    masks = _head_masks((CHUNK, GROUP_WIDTH))
    row_head = lax.broadcasted_iota(jnp.int32, (GROUP_WIDTH, GROUP_WIDTH), 0) // HEAD_DIM
    col_head = lax.broadcasted_iota(jnp.int32, (GROUP_WIDTH, GROUP_WIDTH), 1) // HEAD_DIM
    blockdiag = row_head == col_head
    t_idx = lax.broadcasted_iota(jnp.int32, (CHUNK, N_HEADS * CHUNK), 0)
    s_idx = lax.broadcasted_iota(jnp.int32, (CHUNK, N_HEADS * CHUNK), 1) % CHUNK
    ws = jnp.where(s_idx <= t_idx, wsg_ref[...], 0.0).astype(BF16)
    dmask = dmask_ref[...]
    qdec, kdec, cdec = qdec_ref[...], kdec_ref[...], cdec_ref[...]

    for c in range(tc // CHUNK):
        rows = pl.ds(c * CHUNK, CHUNK)
        mixed = _dot(ws, _stack_heads(vn_ref[rows, :], masks)) + bsg_ref[...]
        oa_ref[rows, :] = (ua_ref[rows, :] * mixed).astype(oa_ref.dtype)
        qr, kr, vc = qr_ref[rows, :], kr_ref[rows, :], vc_ref[rows, :]
        scores = _dot_nt(qr, _stack_heads(kr, masks)) * dmask
        intra = _dot(scores.astype(BF16), _stack_heads(vc, masks))
        state = s_scr[...]
        inter = _dot(qr, state.astype(BF16)) * qdec
        o = intra + inter
        mu = jnp.zeros_like(o)
        for m in masks:
            mu = mu + jnp.where(m, jnp.sum(jnp.where(m, o, 0.0), axis=-1, keepdims=True), 0.0)
        oc = o - mu * (1.0 / HEAD_DIM)
        var = jnp.zeros_like(o)
        sq = oc * oc
        for m in masks:
            var = var + jnp.where(m, jnp.sum(jnp.where(m, sq, 0.0), axis=-1, keepdims=True), 0.0)
        on = oc * lax.rsqrt(var * (1.0 / HEAD_DIM) + EPS)
        oc_ref[rows, :] = (sg_ref[rows, :] * on).astype(oc_ref.dtype)
        kd = (kr.astype(F32) * kdec).astype(BF16)
        upd = _dot_tn(kd, vc)
        s_scr[...] = state * cdec + jnp.where(blockdiag, upd, 0.0)

    buf_scr[CONV_TAIL:CONV_TAIL + tc, :] = glu_ref[...]
    first_tap = CONV_TAIL - (CONV_WIDTH - 1)
    for r0 in range(0, tc, conv_rows):
        acc = jnp.zeros((conv_rows, GROUP_WIDTH), F32) + cb_ref[...]
        for k in range(CONV_WIDTH):
            acc = acc + cw_ref[k:k + 1, :] * buf_scr[pl.ds(r0 + first_tap + k, conv_rows), :]
        y = _standardize(acc) * clg_ref[...] + clb_ref[...]
        od_ref[pl.ds(r0, conv_rows), :] = (y * _sigmoid(y)).astype(od_ref.dtype)
    tail = buf_scr[tc:tc + CONV_TAIL, :]
    buf_scr[0:CONV_TAIL, :] = tail

    @pl.when(j == pl.num_programs(1) - 1)
    def _():
        ret_ref[0] = s_scr[...]
        cst_ref[0] = tail


def _mix(ua, vn, qr, kr, vc, sg, glu, consts, *, batch, seq, tc):
    nj = seq // tc
    row_blk = lambda b, j: (b * nj + j, 0)
    gw = pl.BlockSpec((tc, GROUP_WIDTH), row_blk)
    const_specs = [_resident(c.shape) for c in consts]
    rows = batch * seq
    return pl.pallas_call(
        functools.partial(_mix_kernel, tc=tc, conv_rows=min(64, tc)),
        grid=(batch, nj),
        in_specs=[gw] * 7 + const_specs,
        out_specs=[gw, gw, gw,
                   pl.BlockSpec((1, GROUP_WIDTH, GROUP_WIDTH), lambda b, j: (b, 0, 0)),
                   pl.BlockSpec((1, CONV_TAIL, GROUP_WIDTH), lambda b, j: (b, 0, 0))],
        out_shape=[jax.ShapeDtypeStruct((rows, GROUP_WIDTH), BF16)] * 3
                  + [jax.ShapeDtypeStruct((batch, GROUP_WIDTH, GROUP_WIDTH), F32),
                     jax.ShapeDtypeStruct((batch, CONV_TAIL, GROUP_WIDTH), F32)],
        scratch_shapes=[pltpu.VMEM((GROUP_WIDTH, GROUP_WIDTH), F32),
                        pltpu.VMEM((tc + CONV_TAIL, GROUP_WIDTH), F32)],
        compiler_params=_cparams(2),
        name="mix",
    )(ua, vn, qr, kr, vc, sg, glu, *consts)


def _sb_block(q4, kblk, vblk, bias, tri, carry, mask):
    tq = q4.shape[0] // N_HEADS
    s4 = _dot_nt(q4, kblk)
    z = jnp.concatenate([s4[h * tq:(h + 1) * tq, :] + bias[h] for h in range(N_HEADS)], axis=0)
    sp = _softplus(z)
    if mask is not None:
        sp = jnp.where(mask, sp, 0.0)
    hi, lo = _split_bf16(sp)
    within = _dot(hi, tri) + _dot(lo, tri)
    w = jnp.exp(z - within - carry)
    if mask is not None:
        w = jnp.where(mask, w, 0.0)
    return _dot(w.astype(BF16), vblk), jnp.sum(sp, axis=-1, keepdims=True)


def _attn_kernel(bias_ref, q_ref, k_ref, v_ref, o_ref, acc_scr, c_scr):
    i = pl.program_id(1)
    tq, tk = ATTN_TQ, ATTN_TK
    masks = _head_masks((tq, GROUP_WIDTH))
    q4 = _stack_heads(q_ref[...], masks)
    bias = [bias_ref[h] for h in range(N_HEADS)]
    tri = (lax.broadcasted_iota(jnp.int32, (tk, tk), 0)
           >= lax.broadcasted_iota(jnp.int32, (tk, tk), 1)).astype(BF16)

    jd = lax.div(i * tq, tk)
    col = lax.broadcasted_iota(jnp.int32, (N_HEADS * tq, tk), 1)
    row = lax.broadcasted_iota(jnp.int32, (N_HEADS * tq, tk), 0) % tq
    mask = col < row + (i * tq - jd * tk)
    start = pl.multiple_of(jd * tk, tk)
    o4, tot = _sb_block(q4, k_ref[pl.ds(start, tk), :], v_ref[pl.ds(start, tk), :], bias, tri,
                        jnp.zeros((N_HEADS * tq, tk), F32), mask)
    acc_scr[...] = o4
    c_scr[...] = jnp.zeros(c_scr.shape, F32) + tot

    def body(t, _):
        s0 = pl.multiple_of((jd - 1 - t) * tk, tk)
        o4, tot = _sb_block(q4, k_ref[pl.ds(s0, tk), :], v_ref[pl.ds(s0, tk), :], bias, tri,
                            c_scr[...], None)
        acc_scr[...] += o4
        c_scr[...] += tot
        return 0

    lax.fori_loop(0, jd, body, 0)

    acc = acc_scr[...]
    out = jnp.zeros((tq, GROUP_WIDTH), F32)
    for h, m in enumerate(masks):
        out = out + jnp.where(m, acc[h * tq:(h + 1) * tq, :], 0.0)
    o_ref[...] = out.astype(o_ref.dtype)


def _attn(q, k_bf16, v_bf16, bias, *, batch, seq):
    nq = seq // ATTN_TQ
    rows = batch * seq
    return pl.pallas_call(
        _attn_kernel,
        grid=(batch, nq),
        in_specs=[pl.BlockSpec(memory_space=pltpu.SMEM),
                  pl.BlockSpec((ATTN_TQ, GROUP_WIDTH), lambda b, i: (b * nq + i, 0)),
                  pl.BlockSpec((seq, GROUP_WIDTH), lambda b, i: (b, 0)),
                  pl.BlockSpec((seq, GROUP_WIDTH), lambda b, i: (b, 0))],
        out_specs=pl.BlockSpec((ATTN_TQ, GROUP_WIDTH), lambda b, i: (b * nq + i, 0)),
        out_shape=jax.ShapeDtypeStruct((rows, GROUP_WIDTH), BF16),
        scratch_shapes=[pltpu.VMEM((N_HEADS * ATTN_TQ, GROUP_WIDTH), F32),
                        pltpu.VMEM((N_HEADS * ATTN_TQ, ATTN_TK), F32)],
        compiler_params=_cparams(2),
        name="sb_attn",
    )(bias, q, k_bf16, v_bf16)


def _outffn_kernel(x_ref, a_ref, b_ref, c_ref, d_ref, wo_ref, g_ref, w1_ref, w2_ref, gf_ref,
                   y_ref, *, final_norm):
    mix = jnp.zeros(x_ref.shape, F32)
    for n, m_ref in enumerate((a_ref, b_ref, c_ref, d_ref)):
        mix = mix + _dot(m_ref[...].astype(BF16), wo_ref[n * GROUP_WIDTH:(n + 1) * GROUP_WIDTH, :])
    h = x_ref[...] + mix
    hn = _rms_scale(h, g_ref[...]).astype(BF16)
    acc = jnp.zeros(x_ref.shape, F32)
    for c in range(D_FF // FF_CHUNK):
        a = jnp.maximum(_dot(hn, w1_ref[:, c * FF_CHUNK:(c + 1) * FF_CHUNK]), 0.0)
        acc = acc + _dot((a * a).astype(BF16), w2_ref[c * FF_CHUNK:(c + 1) * FF_CHUNK, :])
    y = h + acc
    y_ref[...] = _rms_scale(y, gf_ref[...]) if final_norm else y


def _outffn(x, a, b, c, d, wo, g, w1, w2, gf, *, tm, final_norm):
    rows = x.shape[0]
    row_blk = lambda i: (i, 0)
    full = pl.BlockSpec((tm, D_MODEL), row_blk)
    gw = pl.BlockSpec((tm, GROUP_WIDTH), row_blk)
    return pl.pallas_call(
        functools.partial(_outffn_kernel, final_norm=final_norm),
        grid=(rows // tm,),
        in_specs=[full, gw, gw, gw, gw,
                  _resident((D_MODEL, D_MODEL)), _resident((1, D_MODEL)),
                  _resident((D_MODEL, D_FF)), _resident((D_FF, D_MODEL)), _resident((1, D_MODEL))],
        out_specs=full,
        out_shape=jax.ShapeDtypeStruct((rows, D_MODEL), F32),
        compiler_params=_cparams(1),
        name="outffn",
    )(x, a, b, c, d, wo, g, w1, w2, gf)


def _smix_kernel(ua_ref, vn_ref, glu_ref, cbuf_ref, w0_ref, b0_ref, cw_ref, cb_ref, clg_ref, clb_ref,
                 oa_ref, od_ref):
    oa_ref[...] = ua_ref[...] * (w0_ref[...] * vn_ref[...] + b0_ref[...])
    hist = cbuf_ref[...] * cw_ref[0:CONV_WIDTH - 1, :][None, :, :]
    y = jnp.sum(hist, axis=1) + cw_ref[CONV_WIDTH - 1:CONV_WIDTH, :] * glu_ref[...] + cb_ref[...]
    y = _standardize(y) * clg_ref[...] + clb_ref[...]
    od_ref[...] = y * _sigmoid(y)


def _smix(ua, vn, glu, cbuf, w0, b0, cw, cb, clg, clb):
    n = ua.shape[0]
    return pl.pallas_call(
        _smix_kernel,
        out_shape=[jax.ShapeDtypeStruct((n, GROUP_WIDTH), F32)] * 2,
        name="smix",
    )(ua, vn, glu, cbuf, w0, b0, cw, cb, clg, clb)


def _sret_kernel(s_ref, qrep_ref, krep_ref, vtile_ref, q_ref, k_ref, v_ref, sg_ref, gam_ref,
                 snew_ref, oc_ref):
    s = s_ref[...]
    gam = gam_ref[...]
    snew_ref[...] = s * gam + krep_ref[...] * vtile_ref[...]
    p = qrep_ref[...] * s
    fold = p[:, 0:2 * HEAD_DIM]
    for m in range(1, HEAD_DIM // 2):
        fold = fold + p[:, m * 2 * HEAD_DIM:(m + 1) * 2 * HEAD_DIM]
    inter = fold[:, 0:HEAD_DIM] + fold[:, HEAD_DIM:2 * HEAD_DIM]
    score = jnp.sum(q_ref[...] * k_ref[...], axis=-1, keepdims=True)
    o = score * v_ref[...] + inter * gam
    oc_ref[...] = sg_ref[...] * _standardize(o)


def _sret(s, qrep, krep, vtile, q, k, v, sg, gam):
    n = s.shape[0]
    return pl.pallas_call(
        _sret_kernel,
        out_shape=[jax.ShapeDtypeStruct(s.shape, F32), jax.ShapeDtypeStruct((n, HEAD_DIM), F32)],
        name="sret",
    )(s, qrep, krep, vtile, q, k, v, sg, gam)


def _sattn_kernel(pt_ref, bias_ref, q_ref, kn_ref, vn_ref, *refs, pages_per_step):
    del pt_ref
    k_refs = refs[:pages_per_step]
    v_refs = refs[pages_per_step:2 * pages_per_step]
    o_ref = refs[2 * pages_per_step]
    acc_scr, c_scr = refs[2 * pages_per_step + 1:]
    j = pl.program_id(1)
    page = k_refs[0].shape[-1]
    rows = 2 * N_HEADS
    lane = lax.broadcasted_iota(jnp.int32, (rows, GROUP_WIDTH), 1)
    rid = lax.broadcasted_iota(jnp.int32, (rows, GROUP_WIDTH), 0)
    own = (lane // HEAD_DIM) == rid
    q8 = jnp.where(own, jnp.broadcast_to(q_ref[0], (rows, GROUP_WIDTH)), 0.0)
    rid_p = lax.broadcasted_iota(jnp.int32, (rows, page), 0)
    bias = jnp.zeros((rows, page), F32)
    for h in range(N_HEADS):
        bias = jnp.where(rid_p == h, bias_ref[h], bias)
    tri = (lax.broadcasted_iota(jnp.int32, (page, page), 0)
           >= lax.broadcasted_iota(jnp.int32, (page, page), 1)).astype(BF16)

    @pl.when(j == 0)
    def _():
        z_new = jnp.sum(q8 * kn_ref[0], axis=-1, keepdims=True) + bias[:, 0:1]
        visible = jnp.full((rows, 1), False)
        sp_new = jnp.where(visible, _softplus(z_new), 0.0)
        w_new = jnp.where(visible, jnp.exp(z_new - sp_new), 0.0)
        acc_scr[...] = w_new * vn_ref[0]
        c_scr[...] = jnp.zeros(c_scr.shape, F32) + sp_new

    acc = acc_scr[...]
    carry = c_scr[...]
    for p in range(pages_per_step - 1, -1, -1):
        kt = k_refs[p][...].reshape(GROUP_WIDTH, page).astype(BF16)
        vt = v_refs[p][...].reshape(GROUP_WIDTH, page).astype(BF16)
        z = _dot(q8.astype(BF16), kt) + bias
        sp = _softplus(z)
        hi, lo = _split_bf16(sp)
        within = _dot(hi, tri) + _dot(lo, tri)
        w = jnp.exp(z - within - carry)
        acc = acc + _dot_nt(w.astype(BF16), vt)
        carry = carry + jnp.sum(sp, axis=-1, keepdims=True)
    acc_scr[...] = acc
    c_scr[...] = carry

    @pl.when(j == pl.num_programs(1) - 1)
    def _():
        o_ref[0] = jnp.sum(jnp.where(own, acc, 0.0), axis=0, keepdims=True)


def _sattn(page_table, bias, q, k_new, v_new, cache_k, cache_v, *, layer, pages_per_step):
    nb, n_pages = page_table.shape
    page = cache_k.shape[-1]
    steps = n_pages // pages_per_step

    def page_spec(p):
        def index_map(b, j, pt):
            return (layer, pt[b, (steps - 1 - j) * pages_per_step + p], 0, 0, 0)
        return pl.BlockSpec((None, None, N_HEADS, HEAD_DIM, page), index_map)

    row_spec = pl.BlockSpec((1, 1, GROUP_WIDTH), lambda b, j, pt: (b, 0, 0))
    grid_spec = pltpu.PrefetchScalarGridSpec(
        num_scalar_prefetch=1,
        grid=(nb, steps),
        in_specs=[pl.BlockSpec(memory_space=pltpu.SMEM), row_spec, row_spec, row_spec]
                 + [page_spec(p) for p in range(pages_per_step)] * 2,
        out_specs=row_spec,
        scratch_shapes=[pltpu.VMEM((2 * N_HEADS, GROUP_WIDTH), F32),
                        pltpu.VMEM((2 * N_HEADS, page), F32)],
    )
    return pl.pallas_call(
        functools.partial(_sattn_kernel, pages_per_step=pages_per_step),
        grid_spec=grid_spec,
        out_shape=jax.ShapeDtypeStruct((nb, 1, GROUP_WIDTH), F32),
        compiler_params=_cparams(2),
        name="paged_sb_attn",
    )(page_table, bias, q, k_new, v_new, *([cache_k] * pages_per_step), *([cache_v] * pages_per_step))


def _rope_tables(pos):
    half = HEAD_DIM // 2
    inv = ROPE_BASE ** (-jnp.arange(0, HEAD_DIM, 2, dtype=F32) / HEAD_DIM)
    ang = pos.astype(F32)[:, None] * inv[None, :]
    cos, sin, zero = jnp.cos(ang), jnp.sin(ang), jnp.zeros_like(ang)
    tile = lambda a, b: jnp.tile(jnp.concatenate([a, b], axis=-1), (1, N_HEADS))
    return tile(cos, cos), tile(-sin, zero), tile(zero, sin)


def _retention_tables(c):
    lg = jnp.log1p(-jnp.exp2(-5.0 - jnp.arange(N_HEADS, dtype=F32)))
    idx = jnp.arange(c, dtype=F32)
    diff = idx[:, None] - idx[None, :]
    dmask = jnp.where(diff >= 0, jnp.exp(jnp.maximum(diff, 0.0) * lg[:, None, None]), 0.0)
    q_decay = jnp.exp((idx[None, :] + 1.0) * lg[:, None])
    k_decay = jnp.exp((c - 1.0 - idx[None, :]) * lg[:, None])
    chunk_decay = jnp.exp(c * lg)
    lanes = lambda hc: jnp.repeat(hc.T, HEAD_DIM, axis=1)
    return (jnp.transpose(dmask, (1, 0, 2)).reshape(c, N_HEADS * c), lanes(q_decay), lanes(k_decay),
            jnp.repeat(chunk_decay, HEAD_DIM)[None, :])


def kernel(x_prompt, x_sample, cache_k, cache_v, state_ret, state_conv, page_table, w_in, w_out, sgu_w, sgu_b, sgu_ln_g, sgu_ln_b, conv_w, conv_b, conv_ln_g, conv_ln_b, sb_bias, norm_mix, norm_ffn, norm_final, w_ff1, w_ff2):
    bp, lp, _ = x_prompt.shape
    bs, ls, _ = x_sample.shape
    assert ls == 1, "sample group is a single-token decode step"
    depth = w_in.shape[0]
    past_len = page_table.shape[1] * cache_k.shape[2]
    rows_p = bp * lp
    tm_p = min(512, rows_p)
    tc = min(512, lp)

    w_in_b, w_out_b = w_in.astype(BF16), w_out.astype(BF16)
    w_ff1_b, w_ff2_b = w_ff1.astype(BF16), w_ff2.astype(BF16)
    cache_kt = jnp.transpose(cache_k, (0, 1, 3, 4, 2))
    cache_vt = jnp.transpose(cache_v, (0, 1, 3, 4, 2))

    rope_p = _rope_tables(jnp.arange(lp))
    rope_s = _rope_tables(jnp.full((bs,), past_len))
    dmask_p, qdec_p, kdec_p, cdec_p = _retention_tables(CHUNK)
    _, qdec_s, _, _ = _retention_tables(1)
    gam_rows = jnp.tile(qdec_s.reshape(N_HEADS, HEAD_DIM)[:, :1], (bs, 1))

    row = lambda a: a[None, :]
    gf = row(norm_final)
    hp = x_prompt.reshape(rows_p, D_MODEL)
    hs = x_sample.reshape(bs, D_MODEL)
    outs = {n: [] for n in ("kp", "vp", "ks", "vs", "retp", "rets", "convp", "convs", "sguv")}

    for l in range(depth):
        ua, vn, q, k, v, qr, kr, vc, sg, glu = _inproj(
            hp, row(norm_mix[l]), w_in_b[l], *rope_p, row(sgu_ln_g[l]), row(sgu_ln_b[l]),
            tm=tm_p, mm_dtype=BF16)
        wsg = jnp.transpose(sgu_w[l], (1, 0, 2)).reshape(CHUNK, N_HEADS * CHUNK)
        bsg = jnp.repeat(sgu_b[l].T, HEAD_DIM, axis=1)
        consts = (wsg, bsg, dmask_p, qdec_p, kdec_p, cdec_p,
                  conv_w[l], row(conv_b[l]), row(conv_ln_g[l]), row(conv_ln_b[l]))
        oa, oc, od, ret_bd, conv_tail = _mix(ua, vn, qr, kr, vc, sg, glu, consts, batch=bp, seq=lp, tc=tc)
        ob = _attn(q, k.astype(BF16), v.astype(BF16), sb_bias[l], batch=bp, seq=lp)
        hp = _outffn(hp, oa, ob, oc, od, w_out_b[l], row(norm_ffn[l]), w_ff1_b[l], w_ff2_b[l], gf,
                     tm=tm_p, final_norm=(l == depth - 1))
        outs["kp"].append(k.reshape(bp, lp, N_HEADS, HEAD_DIM))
        outs["vp"].append(v.reshape(bp, lp, N_HEADS, HEAD_DIM))
        outs["retp"].append(jnp.stack(
            [ret_bd[:, h * HEAD_DIM:(h + 1) * HEAD_DIM, h * HEAD_DIM:(h + 1) * HEAD_DIM] for h in range(N_HEADS)],
            axis=1))
        outs["convp"].append(conv_tail[:, CONV_TAIL - (CONV_WIDTH - 1):, :])

        ua, vn, q, k, v, qr, kr, vc, sg, glu = _inproj(
            hs, row(norm_mix[l]), w_in_b[l], *rope_s, row(sgu_ln_g[l]), row(sgu_ln_b[l]),
            tm=bs, mm_dtype=F32)
        w0 = row(jnp.repeat(sgu_w[l][:, 0, 0], HEAD_DIM))
        b0 = row(jnp.repeat(sgu_b[l][:, 0], HEAD_DIM))
        oa, od = _smix(ua, vn, glu, state_conv[l], w0, b0, conv_w[l], row(conv_b[l]),
                       row(conv_ln_g[l]), row(conv_ln_b[l]))
        per_head = lambda t: t.reshape(bs * N_HEADS, HEAD_DIM)
        q_h, k_h, v_h = per_head(qr), per_head(kr), per_head(vc)
        s_new, oc = _sret(state_ret[l].reshape(bs * N_HEADS, HEAD_DIM * HEAD_DIM),
                          jnp.repeat(q_h, HEAD_DIM, axis=1), jnp.repeat(k_h, HEAD_DIM, axis=1),
                          jnp.tile(v_h, (1, HEAD_DIM)), q_h, k_h, v_h, per_head(sg), gam_rows)
        ob = _sattn(page_table, sb_bias[l], q[:, None, :], k[:, None, :], v[:, None, :], cache_kt, cache_vt,
                    layer=l, pages_per_step=8)
        hs = _outffn(hs, oa, ob.reshape(bs, GROUP_WIDTH), oc.reshape(bs, GROUP_WIDTH), od,
                     w_out_b[l], row(norm_ffn[l]), w_ff1_b[l], w_ff2_b[l], gf,
                     tm=bs, final_norm=(l == depth - 1))
        outs["ks"].append(k.reshape(bs, 1, N_HEADS, HEAD_DIM))
        outs["vs"].append(v.reshape(bs, 1, N_HEADS, HEAD_DIM))
        outs["rets"].append(s_new.reshape(bs, N_HEADS, HEAD_DIM, HEAD_DIM))
        outs["convs"].append(jnp.concatenate([state_conv[l][:, 1:, :], glu[:, None, :]], axis=1))
        outs["sguv"].append(vn[:, None, :])

    st = lambda n: jnp.stack(outs[n])
    return (hp.reshape(bp, lp, D_MODEL), hs.reshape(bs, 1, D_MODEL), st("kp"), st("vp"), st("ks"), st("vs"),
            st("retp"), st("rets"), st("convp"), st("convs"), st("sguv"))
```

```python
import functools
import math

import jax
import jax.numpy as jnp
from jax import lax
from jax.experimental import pallas as pl
from jax.experimental.pallas import tpu as pltpu

F32 = jnp.float32
BF16 = jnp.bfloat16

D_MODEL = 1024
GROUP_WIDTH = 256
N_HEADS = 4
HEAD_DIM = 64
N_SPLITS = 11
CHUNK = 128
CONV_WIDTH = 31
CONV_TAIL = 32
SUBLANES = 8
D_FF = 4 * D_MODEL
FF_CHUNK = 1024
EPS = 1e-6
ROPE_BASE = 10000.0
LOG2E = math.log2(math.e)
QK_SCALE = HEAD_DIM ** -0.5
ATTN_TQ = 256
ATTN_TK = 256
SOFTPLUS2_LINEAR_ABOVE = 30.0 * LOG2E
MASKED_LOGIT = -1e30
VMEM_LIMIT_BYTES = 56 * 1024 * 1024
PAGES_PER_STEP = 16


def _cparams(n_axes):
    return pltpu.CompilerParams(dimension_semantics=("arbitrary",) * n_axes,
                                vmem_limit_bytes=VMEM_LIMIT_BYTES)


def _layer_block(arr, layer):
    nd = arr.ndim - 1
    return pl.BlockSpec((None,) + arr.shape[1:], lambda *_: (layer,) + (0,) * nd,
                        pipeline_mode=pl.Buffered(1))


def _whole(arr):
    nd = arr.ndim
    return pl.BlockSpec(arr.shape, lambda *_: (0,) * nd, pipeline_mode=pl.Buffered(1))


def _gelu_tanh(x):
    return x * (0.5 * (1.0 + jnp.tanh(0.7978845608028654 * (x + 0.044715 * (x * x * x)))))


def _sigmoid(x):
    return 1.0 / (1.0 + jnp.exp(-x))


def _standardize(x):
    mu = jnp.mean(x, axis=-1, keepdims=True)
    xc = x - mu
    return xc * lax.rsqrt(jnp.mean(xc * xc, axis=-1, keepdims=True) + EPS)


def _rms_scale(x, g):
    return x * lax.rsqrt(jnp.mean(x * x, axis=-1, keepdims=True) + EPS) * g


def _softplus2(z):
    return jnp.maximum(z, jnp.log2(1.0 + jnp.exp2(jnp.minimum(z, SOFTPLUS2_LINEAR_ABOVE))))


def _head_masks(shape):
    lane = lax.broadcasted_iota(jnp.int32, shape, len(shape) - 1)
    return [(lane >= h * HEAD_DIM) & (lane < (h + 1) * HEAD_DIM) for h in range(N_HEADS)]


def _stack_heads(x, masks):
    zero = jnp.zeros_like(x)
    return jnp.concatenate([jnp.where(m, x, zero) for m in masks], axis=0)


def _split_bf16(x):
    hi = x.astype(BF16)
    lo = (x - hi.astype(F32)).astype(BF16)
    return hi, lo


def _suffix_ones(n):
    return (lax.broadcasted_iota(jnp.int32, (n, n), 0)
            >= lax.broadcasted_iota(jnp.int32, (n, n), 1)).astype(BF16)


def _dot(a, b):
    return jnp.dot(a, b, preferred_element_type=F32)


def _dot_nt(a, b):
    return lax.dot_general(a, b, (((1,), (1,)), ((), ())), preferred_element_type=F32)


def _dot_tn(a, b):
    return lax.dot_general(a, b, (((0,), (0,)), ((), ())), preferred_element_type=F32)


def _inproj_kernel(x_ref, g_ref, w_ref, cos_ref, s1_ref, s2_ref, lng_ref, lnb_ref, *refs, prompt):
    n_out = 12 if prompt else 10
    outs = refs[len(refs) - n_out:]
    xb = _rms_scale(x_ref[...], g_ref[...]).astype(BF16)

    def proj(i):
        return _dot(xb, w_ref[:, i * GROUP_WIDTH:(i + 1) * GROUP_WIDTH])

    cos, s1, s2 = cos_ref[...], s1_ref[...], s2_ref[...]

    def rope(t):
        return (t * cos + pltpu.roll(t, GROUP_WIDTH - HEAD_DIM // 2, 1) * s1
                + pltpu.roll(t, HEAD_DIM // 2, 1) * s2)

    if prompt:
        ua_ref, vn_ref, q_ref, ktf_ref, vtf_ref, ktb_ref, vb_ref, qr_ref, kr_ref, vc_ref, sg_ref, glu_ref = outs
    else:
        ua_ref, vn_ref, q_ref, k_ref, v_ref, qr_ref, kr_ref, vc_ref, sg_ref, glu_ref = outs

    ua_ref[...] = _gelu_tanh(proj(0)).astype(ua_ref.dtype)
    vn = _standardize(_gelu_tanh(proj(1))) * lng_ref[...] + lnb_ref[...]
    vn_ref[...] = vn.astype(vn_ref.dtype)
    q_ref[...] = (proj(2) * (QK_SCALE * LOG2E)).astype(q_ref.dtype)
    k, v = proj(3), proj(4)
    if prompt:
        tm = k.shape[0]
        kt, vt = k.T, v.T
        ktf_ref[...] = kt.reshape(N_HEADS, HEAD_DIM, tm)
        vtf_ref[...] = vt.reshape(N_HEADS, HEAD_DIM, tm)
        ktb = kt.astype(BF16)
        for c in range(tm // ATTN_TK):
            ktb_ref[c] = ktb[:, c * ATTN_TK:(c + 1) * ATTN_TK]
        vb_ref[...] = v.astype(BF16)
    else:
        k_ref[...] = k
        v_ref[...] = v
    qr_ref[...] = rope(proj(5)).astype(qr_ref.dtype)
    kr_ref[...] = (rope(proj(6)) * QK_SCALE).astype(kr_ref.dtype)
    vc_ref[...] = proj(7).astype(vc_ref.dtype)
    g = proj(8)
    sg_ref[...] = (g * _sigmoid(g)).astype(sg_ref.dtype)
    glu_ref[...] = proj(9) * _sigmoid(proj(10))


def _inproj_prompt(x, g, w, rope, lng, lnb, kv_prev, *, layer, batch, seq, tm):
    rows = batch * seq
    nt = seq // tm
    row_blk = lambda i: (i, 0)
    pos_blk = lambda i: (i % nt, 0)
    gw = pl.BlockSpec((tm, GROUP_WIDTH), row_blk)
    pos = pl.BlockSpec((tm, GROUP_WIDTH), pos_blk)
    kv_final = pl.BlockSpec((None, None, N_HEADS, HEAD_DIM, tm), lambda i: (layer, i // nt, 0, 0, i % nt))
    kv_shape = jax.ShapeDtypeStruct(kv_prev[0].shape, F32)
    nb = tm // ATTN_TK
    in_specs = [pl.BlockSpec((tm, D_MODEL), row_blk), _layer_block(g, layer), _layer_block(w, layer),
                pos, pos, pos, _layer_block(lng, layer), _layer_block(lnb, layer),
                pl.BlockSpec(memory_space=pl.ANY), pl.BlockSpec(memory_space=pl.ANY)]
    args = [x, g, w, *rope, lng, lnb, *kv_prev]
    aliases = {len(args) - 2: 3, len(args) - 1: 4}
    out_specs = [gw, gw, gw, kv_final, kv_final,
                 pl.BlockSpec((nb, GROUP_WIDTH, ATTN_TK), lambda i: (i, 0, 0)), gw, gw, gw, gw, gw, gw]
    gws = lambda dt: jax.ShapeDtypeStruct((rows, GROUP_WIDTH), dt)
    out_shape = [gws(F32), gws(BF16), gws(BF16), kv_shape, kv_shape,
                 jax.ShapeDtypeStruct((rows // ATTN_TK, GROUP_WIDTH, ATTN_TK), BF16), gws(BF16),
                 gws(BF16), gws(BF16), gws(BF16), gws(F32), gws(F32)]
    return pl.pallas_call(
        functools.partial(_inproj_kernel, prompt=True),
        grid=(rows // tm,), in_specs=in_specs, out_specs=out_specs, out_shape=out_shape,
        input_output_aliases=aliases, compiler_params=_cparams(1), name="inproj",
    )(*args)


def _inproj_sample(x, g, w, rope, lng, lnb, *, layer):
    rows = x.shape[0]
    blk = lambda i: (0, 0)
    gw = pl.BlockSpec((rows, GROUP_WIDTH), blk)
    return pl.pallas_call(
        functools.partial(_inproj_kernel, prompt=False),
        grid=(1,),
        in_specs=[pl.BlockSpec((rows, D_MODEL), blk), _layer_block(g, layer), _layer_block(w, layer),
                  gw, gw, gw, _layer_block(lng, layer), _layer_block(lnb, layer)],
        out_specs=[gw] * 10,
        out_shape=[jax.ShapeDtypeStruct((rows, GROUP_WIDTH), F32)] * 10,
        compiler_params=_cparams(1), name="inproj_sample",
    )(x, g, w, *rope, lng, lnb)


def _mix_kernel(ua_ref, vn_ref, qr_ref, kr_ref, vc_ref, sg_ref, glu_ref,
                wsg_ref, bsg_ref, dmask_ref, qdec_ref, kdec_ref, cdec_ref,
                cw_ref, cb_ref, clg_ref, clb_ref,
                oa_ref, oc_ref, od_ref, ret_ref, cst_ref,
                s_scr, buf_scr, shift_scr, *, tc, conv_rows):
    j = pl.program_id(1)

    @pl.when(j == 0)
    def _():
        s_scr[...] = jnp.zeros(s_scr.shape, F32)
        buf_scr[0:CONV_TAIL, :] = jnp.zeros((CONV_TAIL, GROUP_WIDTH), F32)

    masks = _head_masks((CHUNK, GROUP_WIDTH))
    row_head = lax.broadcasted_iota(jnp.int32, (GROUP_WIDTH, GROUP_WIDTH), 0) // HEAD_DIM
    col_head = lax.broadcasted_iota(jnp.int32, (GROUP_WIDTH, GROUP_WIDTH), 1) // HEAD_DIM
    blockdiag = row_head == col_head
    t_idx = lax.broadcasted_iota(jnp.int32, (CHUNK, N_HEADS * CHUNK), 0)
    s_idx = lax.broadcasted_iota(jnp.int32, (CHUNK, N_HEADS * CHUNK), 1) % CHUNK
    ws = jnp.where(s_idx <= t_idx, wsg_ref[...], 0.0).astype(BF16)
    dmask = dmask_ref[...]
    qdec, kdec, cdec = qdec_ref[...], kdec_ref[...], cdec_ref[...]

    for c in range(tc // CHUNK):
        rows = pl.ds(c * CHUNK, CHUNK)
        mixed = _dot(ws, _stack_heads(vn_ref[rows, :], masks)) + bsg_ref[...]
        oa_ref[rows, :] = (ua_ref[rows, :] * mixed).astype(oa_ref.dtype)
        qr, kr, vc = qr_ref[rows, :], kr_ref[rows, :], vc_ref[rows, :]
        scores = _dot_nt(qr, _stack_heads(kr, masks)) * dmask
        intra = _dot(scores.astype(BF16), _stack_heads(vc, masks))
        state = s_scr[...]
        inter = _dot(qr, state.astype(BF16)) * qdec
        o = intra + inter
        mu = jnp.zeros_like(o)
        for m in masks:
            mu = mu + jnp.where(m, jnp.sum(jnp.where(m, o, 0.0), axis=-1, keepdims=True), 0.0)
        oc = o - mu * (1.0 / HEAD_DIM)
        var = jnp.zeros_like(o)
        sq = oc * oc
        for m in masks:
            var = var + jnp.where(m, jnp.sum(jnp.where(m, sq, 0.0), axis=-1, keepdims=True), 0.0)
        on = oc * lax.rsqrt(var * (1.0 / HEAD_DIM) + EPS)
        oc_ref[rows, :] = (sg_ref[rows, :] * on).astype(oc_ref.dtype)
        kd = (kr.astype(F32) * kdec).astype(BF16)
        upd = _dot_tn(kd, vc)
        s_scr[...] = state * cdec + jnp.where(blockdiag, upd, 0.0)

    buf_scr[CONV_TAIL:CONV_TAIL + tc, :] = glu_ref[...]
    span = tc + CONV_TAIL - SUBLANES
    for r in range(1, SUBLANES):
        shift_scr[r - 1, 0:span, :] = buf_scr[pl.ds(r, span), :]
    first_tap = CONV_TAIL - (CONV_WIDTH - 1)
    for r0 in range(0, tc, conv_rows):
        acc = jnp.zeros((conv_rows, GROUP_WIDTH), F32) + cb_ref[...]
        for k in range(CONV_WIDTH):
            off = first_tap + k
            base = r0 + (off // SUBLANES) * SUBLANES
            if off % SUBLANES == 0:
                rows_k = buf_scr[pl.ds(base, conv_rows), :]
            else:
                rows_k = shift_scr[off % SUBLANES - 1, pl.ds(base, conv_rows), :]
            acc = acc + cw_ref[k:k + 1, :] * rows_k
        y = _standardize(acc) * clg_ref[...] + clb_ref[...]
        od_ref[pl.ds(r0, conv_rows), :] = (y * _sigmoid(y)).astype(od_ref.dtype)
    tail = buf_scr[tc:tc + CONV_TAIL, :]
    buf_scr[0:CONV_TAIL, :] = tail

    @pl.when(j == pl.num_programs(1) - 1)
    def _():
        ret_ref[0] = s_scr[...]
        cst_ref[0] = tail


def _mix(ua, vn, qr, kr, vc, sg, glu, layer_consts, shared_consts, *, layer, batch, seq, tc):
    nj = seq // tc
    row_blk = lambda b, j: (b * nj + j, 0)
    gw = pl.BlockSpec((tc, GROUP_WIDTH), row_blk)
    wsg, bsg, cw, cb, clg, clb = layer_consts
    dmask, qdec, kdec, cdec = shared_consts
    const_specs = ([_layer_block(wsg, layer), _layer_block(bsg, layer)]
                   + [_whole(c) for c in shared_consts]
                   + [_layer_block(c, layer) for c in (cw, cb, clg, clb)])
    rows = batch * seq
    return pl.pallas_call(
        functools.partial(_mix_kernel, tc=tc, conv_rows=min(64, tc)),
        grid=(batch, nj),
        in_specs=[gw] * 7 + const_specs,
        out_specs=[gw, gw, gw,
                   pl.BlockSpec((1, GROUP_WIDTH, GROUP_WIDTH), lambda b, j: (b, 0, 0)),
                   pl.BlockSpec((1, CONV_TAIL, GROUP_WIDTH), lambda b, j: (b, 0, 0))],
        out_shape=[jax.ShapeDtypeStruct((rows, GROUP_WIDTH), BF16)] * 3
                  + [jax.ShapeDtypeStruct((batch, GROUP_WIDTH, GROUP_WIDTH), F32),
                     jax.ShapeDtypeStruct((batch, CONV_TAIL, GROUP_WIDTH), F32)],
        scratch_shapes=[pltpu.VMEM((GROUP_WIDTH, GROUP_WIDTH), F32),
                        pltpu.VMEM((tc + CONV_TAIL, GROUP_WIDTH), F32),
                        pltpu.VMEM((SUBLANES - 1, tc + CONV_TAIL, GROUP_WIDTH), F32)],
        compiler_params=_cparams(2),
        name="mix",
    )(ua, vn, qr, kr, vc, sg, glu, wsg, bsg, dmask, qdec, kdec, cdec, cw, cb, clg, clb)


def _attn_kernel(bias_ref, q_ref, ktb_ref, vb_ref, o_ref, vbd_scr, s_scr, zc_scr, within_scr, nxt_scr, acc_scr):
    i = pl.program_id(1)
    tq, tk = ATTN_TQ, ATTN_TK
    nkb = ktb_ref.shape[0]

    @pl.when(i == 0)
    def _():
        kmasks = _head_masks((tk, GROUP_WIDTH))

        def build(jb, _):
            vbd_scr[jb] = _stack_heads(vb_ref[pl.ds(pl.multiple_of(jb * tk, tk), tk), :], kmasks)
            return 0

        lax.fori_loop(0, nkb, build, 0)

    q4 = _stack_heads(q_ref[...], _head_masks((tq, GROUP_WIDTH)))
    bias = [bias_ref[h] * LOG2E for h in range(N_HEADS)]
    tri = _suffix_ones(tk)

    def logits(jb):
        s_scr[...] = _dot(q4, ktb_ref[jb])

    def masses(mask):
        s4 = s_scr[...]
        z = jnp.concatenate([s4[h * tq:(h + 1) * tq, :] + bias[h] for h in range(N_HEADS)], axis=0)
        sp = _softplus2(z)
        if mask is not None:
            sp = jnp.where(mask, sp, 0.0)
            z = jnp.where(mask, z, MASKED_LOGIT)
        later = nxt_scr[...]
        zc_scr[...] = z - jnp.concatenate([later] * (tk // CHUNK), axis=1)
        nxt_scr[...] = later + jnp.sum(sp, axis=-1, keepdims=True)
        within_scr[...] = _dot(sp.astype(BF16), tri)

    def outputs(jb):
        w = jnp.exp2(zc_scr[...] - within_scr[...]).astype(BF16)
        wcat = jnp.concatenate([w[h * tq:(h + 1) * tq, :] for h in range(N_HEADS)], axis=1)
        acc_scr[...] += _dot(wcat, vbd_scr[jb])

    acc_scr[...] = jnp.zeros(acc_scr.shape, F32)
    nxt_scr[...] = jnp.zeros(nxt_scr.shape, F32)
    jd = lax.div(i * tq, tk)
    col = lax.broadcasted_iota(jnp.int32, (N_HEADS * tq, tk), 1)
    row = lax.broadcasted_iota(jnp.int32, (N_HEADS * tq, tk), 0) % tq
    logits(jd)
    masses(col < row + (i * tq - jd * tk))
    logits(jnp.maximum(jd - 1, 0))

    def body(t, _):
        outputs(jd - t)
        masses(None)
        logits(jnp.maximum(jd - t - 2, 0))
        return 0

    lax.fori_loop(0, jd, body, 0)
    outputs(0)
    o_ref[...] = acc_scr[...].astype(o_ref.dtype)


def _attn(q, ktb, vb, bias, *, batch, seq):
    nq = seq // ATTN_TQ
    nkb = seq // ATTN_TK
    rows = batch * seq
    return pl.pallas_call(
        _attn_kernel,
        grid=(batch, nq),
        in_specs=[pl.BlockSpec(memory_space=pltpu.SMEM),
                  pl.BlockSpec((ATTN_TQ, GROUP_WIDTH), lambda b, i: (b * nq + i, 0)),
                  pl.BlockSpec((nkb, GROUP_WIDTH, ATTN_TK), lambda b, i: (b, 0, 0)),
                  pl.BlockSpec((seq, GROUP_WIDTH), lambda b, i: (b, 0))],
        out_specs=pl.BlockSpec((ATTN_TQ, GROUP_WIDTH), lambda b, i: (b * nq + i, 0)),
        out_shape=jax.ShapeDtypeStruct((rows, GROUP_WIDTH), BF16),
        scratch_shapes=[pltpu.VMEM((nkb, N_HEADS * ATTN_TK, GROUP_WIDTH), BF16),
                        pltpu.VMEM((N_HEADS * ATTN_TQ, ATTN_TK), F32),
                        pltpu.VMEM((N_HEADS * ATTN_TQ, ATTN_TK), F32),
                        pltpu.VMEM((N_HEADS * ATTN_TQ, ATTN_TK), F32),
                        pltpu.VMEM((N_HEADS * ATTN_TQ, CHUNK), F32),
                        pltpu.VMEM((ATTN_TQ, GROUP_WIDTH), F32)],
        compiler_params=_cparams(2),
        name="sb_attn",
    )(bias, q, ktb, vb)


def _outffn_kernel(x_ref, a_ref, b_ref, c_ref, d_ref, wo_ref, g_ref, w1_ref, w2_ref, gf_ref,
                   y_ref, *, final_norm):
    mix = jnp.zeros(x_ref.shape, F32)
    for n, m_ref in enumerate((a_ref, b_ref, c_ref, d_ref)):
        mix = mix + _dot(m_ref[...].astype(BF16), wo_ref[n * GROUP_WIDTH:(n + 1) * GROUP_WIDTH, :])
    h = x_ref[...] + mix
    hn = _rms_scale(h, g_ref[...]).astype(BF16)
    acc = jnp.zeros(x_ref.shape, F32)
    for c in range(D_FF // FF_CHUNK):
        a = jnp.maximum(_dot(hn, w1_ref[:, c * FF_CHUNK:(c + 1) * FF_CHUNK]), 0.0)
        acc = acc + _dot((a * a).astype(BF16), w2_ref[c * FF_CHUNK:(c + 1) * FF_CHUNK, :])
    y = h + acc
    y_ref[...] = _rms_scale(y, gf_ref[...]) if final_norm else y


def _outffn(x, a, b, c, d, wo, g, w1, w2, gf, *, layer, tm, final_norm):
    rows = x.shape[0]
    row_blk = lambda i: (i, 0)
    full = pl.BlockSpec((tm, D_MODEL), row_blk)
    gw = pl.BlockSpec((tm, GROUP_WIDTH), row_blk)
    return pl.pallas_call(
        functools.partial(_outffn_kernel, final_norm=final_norm),
        grid=(rows // tm,),
        in_specs=[full, gw, gw, gw, gw, _layer_block(wo, layer), _layer_block(g, layer),
                  _layer_block(w1, layer), _layer_block(w2, layer), _whole(gf)],
        out_specs=full,
        out_shape=jax.ShapeDtypeStruct((rows, D_MODEL), F32),
        compiler_params=_cparams(1),
        name="outffn",
    )(x, a, b, c, d, wo, g, w1, w2, gf)


def _smix_kernel(ua_ref, vn_ref, glu_ref, cbuf_ref, w0_ref, b0_ref, cw_ref, cb_ref, clg_ref, clb_ref,
                 oa_ref, od_ref):
    oa_ref[...] = ua_ref[...] * (w0_ref[...] * vn_ref[...] + b0_ref[...])
    hist = cbuf_ref[...] * cw_ref[0:CONV_WIDTH - 1, :][None, :, :]
    y = jnp.sum(hist, axis=1) + cw_ref[CONV_WIDTH - 1:CONV_WIDTH, :] * glu_ref[...] + cb_ref[...]
    y = _standardize(y) * clg_ref[...] + clb_ref[...]
    od_ref[...] = y * _sigmoid(y)


def _smix(ua, vn, glu, cbuf, w0, b0, cw, cb, clg, clb, *, layer):
    n = ua.shape[0]
    gw = pl.BlockSpec((n, GROUP_WIDTH), lambda i: (0, 0))
    return pl.pallas_call(
        _smix_kernel,
        grid=(1,),
        in_specs=[gw, gw, gw] + [_layer_block(c, layer) for c in (cbuf, w0, b0, cw, cb, clg, clb)],
        out_specs=[gw, gw],
        out_shape=[jax.ShapeDtypeStruct((n, GROUP_WIDTH), F32)] * 2,
        compiler_params=_cparams(1),
        name="smix",
    )(ua, vn, glu, cbuf, w0, b0, cw, cb, clg, clb)


def _sret_kernel(s_ref, qrep_ref, krep_ref, vtile_ref, q_ref, k_ref, v_ref, sg_ref, gam_ref,
                 snew_ref, oc_ref):
    s = s_ref[...]
    gam = gam_ref[...]
    snew_ref[...] = s * gam + krep_ref[...] * vtile_ref[...]
    p = qrep_ref[...] * s
    fold = p[:, 0:2 * HEAD_DIM]
    for m in range(1, HEAD_DIM // 2):
        fold = fold + p[:, m * 2 * HEAD_DIM:(m + 1) * 2 * HEAD_DIM]
    inter = fold[:, 0:HEAD_DIM] + fold[:, HEAD_DIM:2 * HEAD_DIM]
    score = jnp.sum(q_ref[...] * k_ref[...], axis=-1, keepdims=True)
    o = score * v_ref[...] + inter * gam
    oc_ref[...] = sg_ref[...] * _standardize(o)


def _sret(s, qrep, krep, vtile, q, k, v, sg, gam, *, layer):
    n = q.shape[0]
    wide = pl.BlockSpec((n, HEAD_DIM * HEAD_DIM), lambda i: (0, 0))
    narrow = pl.BlockSpec((n, HEAD_DIM), lambda i: (0, 0))
    return pl.pallas_call(
        _sret_kernel,
        grid=(1,),
        in_specs=[_layer_block(s, layer), wide, wide, wide, narrow, narrow, narrow, narrow, _whole(gam)],
        out_specs=[wide, narrow],
        out_shape=[jax.ShapeDtypeStruct((n, HEAD_DIM * HEAD_DIM), F32), jax.ShapeDtypeStruct((n, HEAD_DIM), F32)],
        compiler_params=_cparams(1),
        name="sret",
    )(s, qrep, krep, vtile, q, k, v, sg, gam)


def _sattn_kernel(pt_ref, bias_ref, q_ref, kn_ref, vn_ref, *refs, pages_per_step):
    del pt_ref
    k_refs = refs[:pages_per_step]
    v_refs = refs[pages_per_step:2 * pages_per_step]
    o_ref = refs[2 * pages_per_step]
    acc_scr, c_scr = refs[2 * pages_per_step + 1:]
    j = pl.program_id(1)
    page = k_refs[0].shape[-1]
    span = pages_per_step * page
    rows = 2 * N_HEADS
    lane = lax.broadcasted_iota(jnp.int32, (rows, GROUP_WIDTH), 1)
    rid = lax.broadcasted_iota(jnp.int32, (rows, GROUP_WIDTH), 0)
    own = (lane // HEAD_DIM) == rid
    q8 = jnp.where(own, jnp.broadcast_to(q_ref[0], (rows, GROUP_WIDTH)), 0.0)
    rid_1 = lax.broadcasted_iota(jnp.int32, (rows, 1), 0)
    bias = jnp.zeros((rows, 1), F32)
    for h in range(N_HEADS):
        bias = jnp.where(rid_1 == h, bias_ref[h] * LOG2E, bias)
    tri = _suffix_ones(ATTN_TK)

    @pl.when(j == 0)
    def _():
        z_new = jnp.sum(q8 * kn_ref[0], axis=-1, keepdims=True) + bias
        visible = jnp.full((rows, 1), False)
        sp_new = jnp.where(visible, _softplus2(z_new), 0.0)
        w_new = jnp.where(visible, jnp.exp2(z_new - sp_new), 0.0)
        acc_scr[...] = w_new * vn_ref[0]
        c_scr[...] = jnp.zeros(c_scr.shape, F32) + sp_new

    kt = jnp.concatenate([r[...].reshape(GROUP_WIDTH, page).astype(BF16) for r in k_refs], axis=1)
    vt = jnp.concatenate([r[...].reshape(GROUP_WIDTH, page).astype(BF16) for r in v_refs], axis=1)
    z = _dot(q8.astype(BF16), kt) + bias
    sp = _softplus2(z)
    hi, lo = _split_bf16(sp)
    n_blk = span // ATTN_TK
    blk = lambda a, c: a[:, c * ATTN_TK:(c + 1) * ATTN_TK]
    carry = c_scr[:, 0:1]
    suffix = [None] * n_blk
    for c in range(n_blk - 1, -1, -1):
        suffix[c] = _dot(blk(hi, c), tri) + _dot(blk(lo, c), tri) + carry
        carry = carry + jnp.sum(blk(sp, c), axis=-1, keepdims=True)
    w = jnp.exp2(z - jnp.concatenate(suffix, axis=1))
    acc = acc_scr[...] + _dot_nt(w.astype(BF16), vt)
    acc_scr[...] = acc
    c_scr[...] = jnp.zeros(c_scr.shape, F32) + carry

    @pl.when(j == pl.num_programs(1) - 1)
    def _():
        o_ref[0] = jnp.sum(jnp.where(own, acc, 0.0), axis=0, keepdims=True)


def _sattn(page_table, bias, q, k_new, v_new, cache_k, cache_v, *, layer, pages_per_step):
    nb, n_pages = page_table.shape
    page = cache_k.shape[-1]
    steps = n_pages // pages_per_step

    def page_spec(p):
        def index_map(b, j, pt):
            return (layer, pt[b, (steps - 1 - j) * pages_per_step + p], 0, 0, 0)
        return pl.BlockSpec((None, None, N_HEADS, HEAD_DIM, page), index_map)

    row_spec = pl.BlockSpec((1, 1, GROUP_WIDTH), lambda b, j, pt: (b, 0, 0))
    grid_spec = pltpu.PrefetchScalarGridSpec(
        num_scalar_prefetch=1,
        grid=(nb, steps),
        in_specs=[pl.BlockSpec(memory_space=pltpu.SMEM), row_spec, row_spec, row_spec]
                 + [page_spec(p) for p in range(pages_per_step)] * 2,
        out_specs=row_spec,
        scratch_shapes=[pltpu.VMEM((2 * N_HEADS, GROUP_WIDTH), F32),
                        pltpu.VMEM((2 * N_HEADS, CHUNK), F32)],
    )
    return pl.pallas_call(
        functools.partial(_sattn_kernel, pages_per_step=pages_per_step),
        grid_spec=grid_spec,
        out_shape=jax.ShapeDtypeStruct((nb, 1, GROUP_WIDTH), F32),
        compiler_params=_cparams(2),
        name="paged_sb_attn",
    )(page_table, bias, q, k_new, v_new, *([cache_k] * pages_per_step), *([cache_v] * pages_per_step))


def _rope_tables(pos):
    inv = ROPE_BASE ** (-jnp.arange(0, HEAD_DIM, 2, dtype=F32) / HEAD_DIM)
    ang = pos.astype(F32)[:, None] * inv[None, :]
    cos, sin, zero = jnp.cos(ang), jnp.sin(ang), jnp.zeros_like(ang)
    tile = lambda a, b: jnp.tile(jnp.concatenate([a, b], axis=-1), (1, N_HEADS))
    return tile(cos, cos), tile(-sin, zero), tile(zero, sin)


def _retention_tables(c):
    lg = jnp.log1p(-jnp.exp2(-5.0 - jnp.arange(N_HEADS, dtype=F32)))
    idx = jnp.arange(c, dtype=F32)
    diff = idx[:, None] - idx[None, :]
    dmask = jnp.where(diff >= 0, jnp.exp(jnp.maximum(diff, 0.0) * lg[:, None, None]), 0.0)
    q_decay = jnp.exp((idx[None, :] + 1.0) * lg[:, None])
    k_decay = jnp.exp((c - 1.0 - idx[None, :]) * lg[:, None])
    chunk_decay = jnp.exp(c * lg)
    lanes = lambda hc: jnp.repeat(hc.T, HEAD_DIM, axis=1)
    return (jnp.transpose(dmask, (1, 0, 2)).reshape(c, N_HEADS * c), lanes(q_decay), lanes(k_decay),
            jnp.repeat(chunk_decay, HEAD_DIM)[None, :])


def kernel(x_prompt, x_sample, cache_k, cache_v, state_ret, state_conv, page_table, w_in, w_out, sgu_w, sgu_b, sgu_ln_g, sgu_ln_b, conv_w, conv_b, conv_ln_g, conv_ln_b, sb_bias, norm_mix, norm_ffn, norm_final, w_ff1, w_ff2):
    bp, lp, _ = x_prompt.shape
    bs, ls, _ = x_sample.shape
    assert ls == 1, "sample group is a single-token decode step"
    depth = w_in.shape[0]
    past_len = page_table.shape[1] * cache_k.shape[2]
    rows_p = bp * lp
    tm_p = min(512, lp)
    tc = min(512, lp)

    w_in_b, w_out_b = w_in.astype(BF16), w_out.astype(BF16)
    w_ff1_b, w_ff2_b = w_ff1.astype(BF16), w_ff2.astype(BF16)
    cache_kt = jnp.transpose(cache_k, (0, 1, 3, 4, 2))
    cache_vt = jnp.transpose(cache_v, (0, 1, 3, 4, 2))

    rope_p = _rope_tables(jnp.arange(lp))
    rope_s = _rope_tables(jnp.full((bs,), past_len))
    shared_p = _retention_tables(CHUNK)
    _, qdec_s, _, _ = _retention_tables(1)
    gam_rows = jnp.tile(qdec_s.reshape(N_HEADS, HEAD_DIM)[:, :1], (bs, 1))

    rows3 = lambda a: a[:, None, :]
    norm_mix3, norm_ffn3 = rows3(norm_mix), rows3(norm_ffn)
    lng3, lnb3 = rows3(sgu_ln_g), rows3(sgu_ln_b)
    cb3, clg3, clb3 = rows3(conv_b), rows3(conv_ln_g), rows3(conv_ln_b)
    gf = norm_final[None, :]
    wsg = jnp.transpose(sgu_w, (0, 2, 1, 3)).reshape(depth, CHUNK, N_HEADS * CHUNK)
    bsg = jnp.repeat(jnp.transpose(sgu_b, (0, 2, 1)), HEAD_DIM, axis=2)
    w0 = rows3(jnp.repeat(sgu_w[:, :, 0, 0], HEAD_DIM, axis=1))
    b0 = rows3(jnp.repeat(sgu_b[:, :, 0], HEAD_DIM, axis=1))
    mix_consts = (wsg, bsg, conv_w, cb3, clg3, clb3)
    state_ret_rows = state_ret.reshape(depth, bs * N_HEADS, HEAD_DIM * HEAD_DIM)

    hp = x_prompt.reshape(rows_p, D_MODEL)
    hs = x_sample.reshape(bs, D_MODEL)
    kv_stack = jnp.zeros((depth, bp, N_HEADS, HEAD_DIM, lp), F32)
    kv_prompt = (kv_stack, kv_stack)
    outs = {n: [] for n in ("ks", "vs", "retp", "rets", "convp", "convs", "sguv")}

    for l in range(depth):
        last = l == depth - 1
        ua, vn, q, ktf, vtf, ktb, vb, qr, kr, vc, sg, glu = _inproj_prompt(
            hp, norm_mix3, w_in_b, rope_p, lng3, lnb3, kv_prompt, layer=l, batch=bp, seq=lp, tm=tm_p)
        kv_prompt = (ktf, vtf)
        oa, oc, od, ret_bd, conv_tail = _mix(ua, vn, qr, kr, vc, sg, glu, mix_consts, shared_p,
                                             layer=l, batch=bp, seq=lp, tc=tc)
        ob = _attn(q, ktb, vb, sb_bias[l], batch=bp, seq=lp)
        hp = _outffn(hp, oa, ob, oc, od, w_out_b, norm_ffn3, w_ff1_b, w_ff2_b, gf,
                     layer=l, tm=tm_p, final_norm=last)
        outs["retp"].append(jnp.stack(
            [ret_bd[:, h * HEAD_DIM:(h + 1) * HEAD_DIM, h * HEAD_DIM:(h + 1) * HEAD_DIM] for h in range(N_HEADS)],
            axis=1))
        outs["convp"].append(conv_tail[:, CONV_TAIL - (CONV_WIDTH - 1):, :])

        ua, vn, q, k, v, qr, kr, vc, sg, glu = _inproj_sample(
            hs, norm_mix3, w_in_b, rope_s, lng3, lnb3, layer=l)
        oa, od = _smix(ua, vn, glu, state_conv, w0, b0, conv_w, cb3, clg3, clb3, layer=l)
        per_head = lambda t: t.reshape(bs * N_HEADS, HEAD_DIM)
        q_h, k_h, v_h = per_head(qr), per_head(kr), per_head(vc)
        s_new, oc = _sret(state_ret_rows, jnp.repeat(q_h, HEAD_DIM, axis=1), jnp.repeat(k_h, HEAD_DIM, axis=1),
                          jnp.tile(v_h, (1, HEAD_DIM)), q_h, k_h, v_h, per_head(sg), gam_rows, layer=l)
        ob = _sattn(page_table, sb_bias[l], q[:, None, :], k[:, None, :], v[:, None, :], cache_kt, cache_vt,
                    layer=l, pages_per_step=min(PAGES_PER_STEP, page_table.shape[1]))
        hs = _outffn(hs, oa, ob.reshape(bs, GROUP_WIDTH), oc.reshape(bs, GROUP_WIDTH), od,
                     w_out_b, norm_ffn3, w_ff1_b, w_ff2_b, gf, layer=l, tm=bs, final_norm=last)
        outs["ks"].append(k.reshape(bs, 1, N_HEADS, HEAD_DIM))
        outs["vs"].append(v.reshape(bs, 1, N_HEADS, HEAD_DIM))
        outs["rets"].append(s_new.reshape(bs, N_HEADS, HEAD_DIM, HEAD_DIM))
        outs["convs"].append(jnp.concatenate([state_conv[l][:, 1:, :], glu[:, None, :]], axis=1))
        outs["sguv"].append(vn[:, None, :])

    st = lambda n: jnp.stack(outs[n])
    kp, vp = (jnp.transpose(t, (0, 1, 4, 2, 3)) for t in kv_prompt)
    return (hp.reshape(bp, lp, D_MODEL), hs.reshape(bs, 1, D_MODEL), kp, vp, st("ks"), st("vs"),
            st("retp"), st("rets"), st("convp"), st("convs"), st("sguv"))
```

```python
import functools
import math

import jax
import jax.numpy as jnp
from jax import lax
from jax.experimental import pallas as pl
from jax.experimental.pallas import tpu as pltpu

F32 = jnp.float32
BF16 = jnp.bfloat16

D_MODEL = 1024
GROUP_WIDTH = 256
N_HEADS = 4
HEAD_DIM = 64
N_SPLITS = 11
CHUNK = 128
CONV_WIDTH = 31
CONV_TAIL = 32
SUBLANES = 8
D_FF = 4 * D_MODEL
FF_CHUNK = 1024
EPS = 1e-6
ROPE_BASE = 10000.0
LOG2E = math.log2(math.e)
QK_SCALE = HEAD_DIM ** -0.5
ATTN_TQ = 256
ATTN_TK = 256
SOFTPLUS2_LINEAR_ABOVE = 30.0 * LOG2E
MASKED_LOGIT = -1e30
VMEM_LIMIT_BYTES = 56 * 1024 * 1024
PAGES_PER_STEP = 16


def _cparams(n_axes):
    return pltpu.CompilerParams(dimension_semantics=("arbitrary",) * n_axes,
                                vmem_limit_bytes=VMEM_LIMIT_BYTES)


def _layer_block(arr, layer):
    nd = arr.ndim - 1
    return pl.BlockSpec((None,) + arr.shape[1:], lambda *_: (layer,) + (0,) * nd,
                        pipeline_mode=pl.Buffered(1))


def _whole(arr):
    nd = arr.ndim
    return pl.BlockSpec(arr.shape, lambda *_: (0,) * nd, pipeline_mode=pl.Buffered(1))


def _gelu_tanh(x):
    return x * (0.5 * (1.0 + jnp.tanh(0.7978845608028654 * (x + 0.044715 * (x * x * x)))))


def _sigmoid(x):
    return 1.0 / (1.0 + jnp.exp(-x))


def _standardize(x):
    mu = jnp.mean(x, axis=-1, keepdims=True)
    xc = x - mu
    return xc * lax.rsqrt(jnp.mean(xc * xc, axis=-1, keepdims=True) + EPS)


def _rms_scale(x, g):
    return x * lax.rsqrt(jnp.mean(x * x, axis=-1, keepdims=True) + EPS) * g


def _softplus2(z):
    return jnp.maximum(z, jnp.log2(1.0 + jnp.exp2(jnp.minimum(z, SOFTPLUS2_LINEAR_ABOVE))))


def _head_masks(shape):
    lane = lax.broadcasted_iota(jnp.int32, shape, len(shape) - 1)
    return [(lane >= h * HEAD_DIM) & (lane < (h + 1) * HEAD_DIM) for h in range(N_HEADS)]


def _stack_heads(x, masks):
    zero = jnp.zeros_like(x)
    return jnp.concatenate([jnp.where(m, x, zero) for m in masks], axis=0)


def _split_bf16(x):
    hi = x.astype(BF16)
    lo = (x - hi.astype(F32)).astype(BF16)
    return hi, lo


def _suffix_ones(n):
    return (lax.broadcasted_iota(jnp.int32, (n, n), 0)
            >= lax.broadcasted_iota(jnp.int32, (n, n), 1)).astype(BF16)


def _dot(a, b):
    return jnp.dot(a, b, preferred_element_type=F32)


def _dot_nt(a, b):
    return lax.dot_general(a, b, (((1,), (1,)), ((), ())), preferred_element_type=F32)


def _dot_tn(a, b):
    return lax.dot_general(a, b, (((0,), (0,)), ((), ())), preferred_element_type=F32)


def _inproj_kernel(x_ref, g_ref, w_ref, cos_ref, s1_ref, s2_ref, lng_ref, lnb_ref, *refs, prompt):
    n_out = 12 if prompt else 10
    outs = refs[len(refs) - n_out:]
    xb = _rms_scale(x_ref[...], g_ref[...]).astype(BF16)

    def proj(i):
        return _dot(xb, w_ref[:, i * GROUP_WIDTH:(i + 1) * GROUP_WIDTH])

    cos, s1, s2 = cos_ref[...], s1_ref[...], s2_ref[...]

    def rope(t):
        return (t * cos + pltpu.roll(t, GROUP_WIDTH - HEAD_DIM // 2, 1) * s1
                + pltpu.roll(t, HEAD_DIM // 2, 1) * s2)

    if prompt:
        ua_ref, vn_ref, q_ref, ktf_ref, vtf_ref, ktb_ref, vb_ref, qr_ref, kr_ref, vc_ref, sg_ref, glu_ref = outs
    else:
        ua_ref, vn_ref, q_ref, k_ref, v_ref, qr_ref, kr_ref, vc_ref, sg_ref, glu_ref = outs

    ua_ref[...] = _gelu_tanh(proj(0)).astype(ua_ref.dtype)
    vn = _standardize(_gelu_tanh(proj(1))) * lng_ref[...] + lnb_ref[...]
    vn_ref[...] = vn.astype(vn_ref.dtype)
    q_ref[...] = (proj(2) * (QK_SCALE * LOG2E)).astype(q_ref.dtype)
    k, v = proj(3), proj(4)
    if prompt:
        tm = k.shape[0]
        kt, vt = k.T, v.T
        ktf_ref[...] = kt.reshape(N_HEADS, HEAD_DIM, tm)
        vtf_ref[...] = vt.reshape(N_HEADS, HEAD_DIM, tm)
        ktb = kt.astype(BF16)
        for c in range(tm // ATTN_TK):
            ktb_ref[c] = ktb[:, c * ATTN_TK:(c + 1) * ATTN_TK]
        vb_ref[...] = v.astype(BF16)
    else:
        k_ref[...] = k
        v_ref[...] = v
    qr_ref[...] = rope(proj(5)).astype(qr_ref.dtype)
    kr_ref[...] = (rope(proj(6)) * QK_SCALE).astype(kr_ref.dtype)
    vc_ref[...] = proj(7).astype(vc_ref.dtype)
    g = proj(8)
    sg_ref[...] = (g * _sigmoid(g)).astype(sg_ref.dtype)
    glu_ref[...] = proj(9) * _sigmoid(proj(10))


def _inproj_prompt(x, g, w, rope, lng, lnb, kv_prev, *, layer, batch, seq, tm):
    rows = batch * seq
    nt = seq // tm
    row_blk = lambda i: (i, 0)
    pos_blk = lambda i: (i % nt, 0)
    gw = pl.BlockSpec((tm, GROUP_WIDTH), row_blk)
    pos = pl.BlockSpec((tm, GROUP_WIDTH), pos_blk)
    kv_final = pl.BlockSpec((None, None, N_HEADS, HEAD_DIM, tm), lambda i: (layer, i // nt, 0, 0, i % nt))
    kv_shape = jax.ShapeDtypeStruct(kv_prev[0].shape, F32)
    nb = tm // ATTN_TK
    in_specs = [pl.BlockSpec((tm, D_MODEL), row_blk), _layer_block(g, layer), _layer_block(w, layer),
                pos, pos, pos, _layer_block(lng, layer), _layer_block(lnb, layer),
                pl.BlockSpec(memory_space=pl.ANY), pl.BlockSpec(memory_space=pl.ANY)]
    args = [x, g, w, *rope, lng, lnb, *kv_prev]
    aliases = {len(args) - 2: 3, len(args) - 1: 4}
    out_specs = [gw, gw, gw, kv_final, kv_final,
                 pl.BlockSpec((nb, GROUP_WIDTH, ATTN_TK), lambda i: (i, 0, 0)), gw, gw, gw, gw, gw, gw]
    gws = lambda dt: jax.ShapeDtypeStruct((rows, GROUP_WIDTH), dt)
    out_shape = [gws(F32), gws(BF16), gws(BF16), kv_shape, kv_shape,
                 jax.ShapeDtypeStruct((rows // ATTN_TK, GROUP_WIDTH, ATTN_TK), BF16), gws(BF16),
                 gws(BF16), gws(BF16), gws(BF16), gws(F32), gws(F32)]
    return pl.pallas_call(
        functools.partial(_inproj_kernel, prompt=True),
        grid=(rows // tm,), in_specs=in_specs, out_specs=out_specs, out_shape=out_shape,
        input_output_aliases=aliases, compiler_params=_cparams(1), name="inproj",
    )(*args)


def _inproj_sample(x, g, w, rope, lng, lnb, *, layer):
    rows = x.shape[0]
    blk = lambda i: (0, 0)
    gw = pl.BlockSpec((rows, GROUP_WIDTH), blk)
    return pl.pallas_call(
        functools.partial(_inproj_kernel, prompt=False),
        grid=(1,),
        in_specs=[pl.BlockSpec((rows, D_MODEL), blk), _layer_block(g, layer), _layer_block(w, layer),
                  gw, gw, gw, _layer_block(lng, layer), _layer_block(lnb, layer)],
        out_specs=[gw] * 10,
        out_shape=[jax.ShapeDtypeStruct((rows, GROUP_WIDTH), F32)] * 10,
        compiler_params=_cparams(1), name="inproj_sample",
    )(x, g, w, *rope, lng, lnb)


def _mix_kernel(ua_ref, vn_ref, qr_ref, kr_ref, vc_ref, sg_ref, glu_ref,
                wsg_ref, bsg_ref, dmask_ref, qdec_ref, kdec_ref, cdec_ref,
                cw_ref, cb_ref, clg_ref, clb_ref,
                oa_ref, oc_ref, od_ref, ret_ref, cst_ref,
                s_scr, buf_scr, shift_scr, *, tc, conv_rows):
    j = pl.program_id(1)

    @pl.when(j == 0)
    def _():
        s_scr[...] = jnp.zeros(s_scr.shape, F32)
        buf_scr[0:CONV_TAIL, :] = jnp.zeros((CONV_TAIL, GROUP_WIDTH), F32)

    masks = _head_masks((CHUNK, GROUP_WIDTH))
    row_head = lax.broadcasted_iota(jnp.int32, (GROUP_WIDTH, GROUP_WIDTH), 0) // HEAD_DIM
    col_head = lax.broadcasted_iota(jnp.int32, (GROUP_WIDTH, GROUP_WIDTH), 1) // HEAD_DIM
    blockdiag = row_head == col_head
    t_idx = lax.broadcasted_iota(jnp.int32, (CHUNK, N_HEADS * CHUNK), 0)
    s_idx = lax.broadcasted_iota(jnp.int32, (CHUNK, N_HEADS * CHUNK), 1) % CHUNK
    ws = jnp.where(s_idx <= t_idx, wsg_ref[...], 0.0).astype(BF16)
    dmask = dmask_ref[...]
    qdec, kdec, cdec = qdec_ref[...], kdec_ref[...], cdec_ref[...]

    for c in range(tc // CHUNK):
        rows = pl.ds(c * CHUNK, CHUNK)
        mixed = _dot(ws, _stack_heads(vn_ref[rows, :], masks)) + bsg_ref[...]
        oa_ref[rows, :] = (ua_ref[rows, :] * mixed).astype(oa_ref.dtype)
        qr, kr, vc = qr_ref[rows, :], kr_ref[rows, :], vc_ref[rows, :]
        scores = _dot_nt(qr, _stack_heads(kr, masks)) * dmask
        intra = _dot(scores.astype(BF16), _stack_heads(vc, masks))
        state = s_scr[...]
        inter = _dot(qr, state.astype(BF16)) * qdec
        o = intra + inter
        mu = jnp.zeros_like(o)
        for m in masks:
            mu = mu + jnp.where(m, jnp.sum(jnp.where(m, o, 0.0), axis=-1, keepdims=True), 0.0)
        oc = o - mu * (1.0 / HEAD_DIM)
        var = jnp.zeros_like(o)
        sq = oc * oc
        for m in masks:
            var = var + jnp.where(m, jnp.sum(jnp.where(m, sq, 0.0), axis=-1, keepdims=True), 0.0)
        on = oc * lax.rsqrt(var * (1.0 / HEAD_DIM) + EPS)
        oc_ref[rows, :] = (sg_ref[rows, :] * on).astype(oc_ref.dtype)
        kd = (kr.astype(F32) * kdec).astype(BF16)
        upd = _dot_tn(kd, vc)
        s_scr[...] = state * cdec + jnp.where(blockdiag, upd, 0.0)

    buf_scr[CONV_TAIL:CONV_TAIL + tc, :] = glu_ref[...]
    span = tc + CONV_TAIL - SUBLANES
    for r in range(1, SUBLANES):
        shift_scr[r - 1, 0:span, :] = buf_scr[pl.ds(r, span), :]
    first_tap = CONV_TAIL - (CONV_WIDTH - 1)
    for r0 in range(0, tc, conv_rows):
        acc = jnp.zeros((conv_rows, GROUP_WIDTH), F32) + cb_ref[...]
        for k in range(CONV_WIDTH):
            off = first_tap + k
            base = r0 + (off // SUBLANES) * SUBLANES
            if off % SUBLANES == 0:
                rows_k = buf_scr[pl.ds(base, conv_rows), :]
            else:
                rows_k = shift_scr[off % SUBLANES - 1, pl.ds(base, conv_rows), :]
            acc = acc + cw_ref[k:k + 1, :] * rows_k
        y = _standardize(acc) * clg_ref[...] + clb_ref[...]
        od_ref[pl.ds(r0, conv_rows), :] = (y * _sigmoid(y)).astype(od_ref.dtype)
    tail = buf_scr[tc:tc + CONV_TAIL, :]
    buf_scr[0:CONV_TAIL, :] = tail

    @pl.when(j == pl.num_programs(1) - 1)
    def _():
        ret_ref[0] = s_scr[...]
        cst_ref[0] = tail


def _mix(ua, vn, qr, kr, vc, sg, glu, layer_consts, shared_consts, *, layer, batch, seq, tc):
    nj = seq // tc
    row_blk = lambda b, j: (b * nj + j, 0)
    gw = pl.BlockSpec((tc, GROUP_WIDTH), row_blk)
    wsg, bsg, cw, cb, clg, clb = layer_consts
    dmask, qdec, kdec, cdec = shared_consts
    const_specs = ([_layer_block(wsg, layer), _layer_block(bsg, layer)]
                   + [_whole(c) for c in shared_consts]
                   + [_layer_block(c, layer) for c in (cw, cb, clg, clb)])
    rows = batch * seq
    return pl.pallas_call(
        functools.partial(_mix_kernel, tc=tc, conv_rows=min(64, tc)),
        grid=(batch, nj),
        in_specs=[gw] * 7 + const_specs,
        out_specs=[gw, gw, gw,
                   pl.BlockSpec((1, GROUP_WIDTH, GROUP_WIDTH), lambda b, j: (b, 0, 0)),
                   pl.BlockSpec((1, CONV_TAIL, GROUP_WIDTH), lambda b, j: (b, 0, 0))],
        out_shape=[jax.ShapeDtypeStruct((rows, GROUP_WIDTH), BF16)] * 3
                  + [jax.ShapeDtypeStruct((batch, GROUP_WIDTH, GROUP_WIDTH), F32),
                     jax.ShapeDtypeStruct((batch, CONV_TAIL, GROUP_WIDTH), F32)],
        scratch_shapes=[pltpu.VMEM((GROUP_WIDTH, GROUP_WIDTH), F32),
                        pltpu.VMEM((tc + CONV_TAIL, GROUP_WIDTH), F32),
                        pltpu.VMEM((SUBLANES - 1, tc + CONV_TAIL, GROUP_WIDTH), F32)],
        compiler_params=_cparams(2),
        name="mix",
    )(ua, vn, qr, kr, vc, sg, glu, wsg, bsg, dmask, qdec, kdec, cdec, cw, cb, clg, clb)


def _attn_kernel(bias_ref, q_ref, ktb_ref, vb_ref, o_ref, vbd_scr, s_scr, zc_scr, within_scr, nxt_scr, acc_scr):
    i = pl.program_id(1)
    tq, tk = ATTN_TQ, ATTN_TK
    nkb = ktb_ref.shape[0]

    @pl.when(i == 0)
    def _():
        kmasks = _head_masks((tk, GROUP_WIDTH))

        def build(jb, _):
            vbd_scr[jb] = _stack_heads(vb_ref[pl.ds(pl.multiple_of(jb * tk, tk), tk), :], kmasks)
            return 0

        lax.fori_loop(0, nkb, build, 0)

    q4 = _stack_heads(q_ref[...], _head_masks((tq, GROUP_WIDTH)))
    bias = [bias_ref[h] * LOG2E for h in range(N_HEADS)]
    tri = _suffix_ones(tk)

    def logits(jb):
        s_scr[...] = _dot(q4, ktb_ref[jb])

    def masses(mask):
        s4 = s_scr[...]
        z = jnp.concatenate([s4[h * tq:(h + 1) * tq, :] + bias[h] for h in range(N_HEADS)], axis=0)
        sp = _softplus2(z)
        if mask is not None:
            sp = jnp.where(mask, sp, 0.0)
            z = jnp.where(mask, z, MASKED_LOGIT)
        later = nxt_scr[...]
        zc_scr[...] = z - jnp.concatenate([later] * (tk // CHUNK), axis=1)
        nxt_scr[...] = later + jnp.sum(sp, axis=-1, keepdims=True)
        within_scr[...] = _dot(sp.astype(BF16), tri)

    def outputs(jb):
        w = jnp.exp2(zc_scr[...] - within_scr[...]).astype(BF16)
        wcat = jnp.concatenate([w[h * tq:(h + 1) * tq, :] for h in range(N_HEADS)], axis=1)
        acc_scr[...] += _dot(wcat, vbd_scr[jb])

    acc_scr[...] = jnp.zeros(acc_scr.shape, F32)
    nxt_scr[...] = jnp.zeros(nxt_scr.shape, F32)
    jd = lax.div(i * tq, tk)
    col = lax.broadcasted_iota(jnp.int32, (N_HEADS * tq, tk), 1)
    row = lax.broadcasted_iota(jnp.int32, (N_HEADS * tq, tk), 0) % tq
    logits(jd)
    masses(col < row + (i * tq - jd * tk))
    logits(jnp.maximum(jd - 1, 0))

    def body(t, _):
        outputs(jd - t)
        masses(None)
        logits(jnp.maximum(jd - t - 2, 0))
        return 0

    lax.fori_loop(0, jd, body, 0)
    outputs(0)
    o_ref[...] = acc_scr[...].astype(o_ref.dtype)


def _attn(q, ktb, vb, bias, *, batch, seq):
    nq = seq // ATTN_TQ
    nkb = seq // ATTN_TK
    rows = batch * seq
    return pl.pallas_call(
        _attn_kernel,
        grid=(batch, nq),
        in_specs=[pl.BlockSpec(memory_space=pltpu.SMEM),
                  pl.BlockSpec((ATTN_TQ, GROUP_WIDTH), lambda b, i: (b * nq + i, 0)),
                  pl.BlockSpec((nkb, GROUP_WIDTH, ATTN_TK), lambda b, i: (b, 0, 0)),
                  pl.BlockSpec((seq, GROUP_WIDTH), lambda b, i: (b, 0))],
        out_specs=pl.BlockSpec((ATTN_TQ, GROUP_WIDTH), lambda b, i: (b * nq + i, 0)),
        out_shape=jax.ShapeDtypeStruct((rows, GROUP_WIDTH), BF16),
        scratch_shapes=[pltpu.VMEM((nkb, N_HEADS * ATTN_TK, GROUP_WIDTH), BF16),
                        pltpu.VMEM((N_HEADS * ATTN_TQ, ATTN_TK), F32),
                        pltpu.VMEM((N_HEADS * ATTN_TQ, ATTN_TK), F32),
                        pltpu.VMEM((N_HEADS * ATTN_TQ, ATTN_TK), F32),
                        pltpu.VMEM((N_HEADS * ATTN_TQ, CHUNK), F32),
                        pltpu.VMEM((ATTN_TQ, GROUP_WIDTH), F32)],
        compiler_params=_cparams(2),
        name="sb_attn",
    )(bias, q, ktb, vb)


def _outffn_kernel(x_ref, a_ref, b_ref, c_ref, d_ref, wo_ref, g_ref, w1_ref, w2_ref, gf_ref,
                   y_ref, *, final_norm, before_chunk=None):
    mix = jnp.zeros(x_ref.shape, F32)
    for n, m_ref in enumerate((a_ref, b_ref, c_ref, d_ref)):
        mix = mix + _dot(m_ref[...].astype(BF16), wo_ref[n * GROUP_WIDTH:(n + 1) * GROUP_WIDTH, :])
    h = x_ref[...] + mix
    hn = _rms_scale(h, g_ref[...]).astype(BF16)
    acc = jnp.zeros(x_ref.shape, F32)
    for c in range(D_FF // FF_CHUNK):
        if before_chunk is not None:
            before_chunk(c)
        a = jnp.maximum(_dot(hn, w1_ref[:, c * FF_CHUNK:(c + 1) * FF_CHUNK]), 0.0)
        acc = acc + _dot((a * a).astype(BF16), w2_ref[c * FF_CHUNK:(c + 1) * FF_CHUNK, :])
    y = h + acc
    y_ref[...] = _rms_scale(y, gf_ref[...]) if final_norm else y


def _outffn(x, a, b, c, d, wo, g, w1, w2, gf, *, layer, tm, final_norm):
    rows = x.shape[0]
    row_blk = lambda i: (i, 0)
    full = pl.BlockSpec((tm, D_MODEL), row_blk)
    gw = pl.BlockSpec((tm, GROUP_WIDTH), row_blk)
    return pl.pallas_call(
        functools.partial(_outffn_kernel, final_norm=final_norm),
        grid=(rows // tm,),
        in_specs=[full, gw, gw, gw, gw, _layer_block(wo, layer), _layer_block(g, layer),
                  _layer_block(w1, layer), _layer_block(w2, layer), _whole(gf)],
        out_specs=full,
        out_shape=jax.ShapeDtypeStruct((rows, D_MODEL), F32),
        compiler_params=_cparams(1),
        name="outffn",
    )(x, a, b, c, d, wo, g, w1, w2, gf)


def _smix_kernel(ua_ref, vn_ref, glu_ref, cbuf_ref, w0_ref, b0_ref, cw_ref, cb_ref, clg_ref, clb_ref,
                 oa_ref, od_ref):
    oa_ref[...] = ua_ref[...] * (w0_ref[...] * vn_ref[...] + b0_ref[...])
    hist = cbuf_ref[...] * cw_ref[0:CONV_WIDTH - 1, :][None, :, :]
    y = jnp.sum(hist, axis=1) + cw_ref[CONV_WIDTH - 1:CONV_WIDTH, :] * glu_ref[...] + cb_ref[...]
    y = _standardize(y) * clg_ref[...] + clb_ref[...]
    od_ref[...] = y * _sigmoid(y)


def _smix(ua, vn, glu, cbuf, w0, b0, cw, cb, clg, clb, *, layer):
    n = ua.shape[0]
    gw = pl.BlockSpec((n, GROUP_WIDTH), lambda i: (0, 0))
    return pl.pallas_call(
        _smix_kernel,
        grid=(1,),
        in_specs=[gw, gw, gw] + [_layer_block(c, layer) for c in (cbuf, w0, b0, cw, cb, clg, clb)],
        out_specs=[gw, gw],
        out_shape=[jax.ShapeDtypeStruct((n, GROUP_WIDTH), F32)] * 2,
        compiler_params=_cparams(1),
        name="smix",
    )(ua, vn, glu, cbuf, w0, b0, cw, cb, clg, clb)


def _sret_kernel(s_ref, qrep_ref, krep_ref, vtile_ref, q_ref, k_ref, v_ref, sg_ref, gam_ref,
                 snew_ref, oc_ref):
    s = s_ref[...]
    gam = gam_ref[...]
    snew_ref[...] = s * gam + krep_ref[...] * vtile_ref[...]
    p = qrep_ref[...] * s
    fold = p[:, 0:2 * HEAD_DIM]
    for m in range(1, HEAD_DIM // 2):
        fold = fold + p[:, m * 2 * HEAD_DIM:(m + 1) * 2 * HEAD_DIM]
    inter = fold[:, 0:HEAD_DIM] + fold[:, HEAD_DIM:2 * HEAD_DIM]
    score = jnp.sum(q_ref[...] * k_ref[...], axis=-1, keepdims=True)
    o = score * v_ref[...] + inter * gam
    oc_ref[...] = sg_ref[...] * _standardize(o)


def _sret(s, qrep, krep, vtile, q, k, v, sg, gam, *, layer):
    n = q.shape[0]
    wide = pl.BlockSpec((n, HEAD_DIM * HEAD_DIM), lambda i: (0, 0))
    narrow = pl.BlockSpec((n, HEAD_DIM), lambda i: (0, 0))
    return pl.pallas_call(
        _sret_kernel,
        grid=(1,),
        in_specs=[_layer_block(s, layer), wide, wide, wide, narrow, narrow, narrow, narrow, _whole(gam)],
        out_specs=[wide, narrow],
        out_shape=[jax.ShapeDtypeStruct((n, HEAD_DIM * HEAD_DIM), F32), jax.ShapeDtypeStruct((n, HEAD_DIM), F32)],
        compiler_params=_cparams(1),
        name="sret",
    )(s, qrep, krep, vtile, q, k, v, sg, gam)


def _page_copies(pt_ref, kc_ref, vc_ref, kbuf, vbuf, sems, unit, slot, *, layer, pages, units_per_seq):
    seq = lax.div(unit, units_per_seq)
    first = (units_per_seq - 1 - lax.rem(unit, units_per_seq)) * pages
    copies = []
    for p in range(pages):
        page_id = pt_ref[seq, first + p]
        copies.append(pltpu.make_async_copy(kc_ref.at[layer, page_id], kbuf.at[slot, p], sems.at[0, slot]))
        copies.append(pltpu.make_async_copy(vc_ref.at[layer, page_id], vbuf.at[slot, p], sems.at[1, slot]))
    return copies


def _head_dots(a, b):
    prod = a * b
    part = jnp.sum(prod.reshape(HEAD_DIM // SUBLANES, SUBLANES, prod.shape[-1]), axis=0)
    for shift in (4, 2, 1):
        part = part + pltpu.roll(part, shift, 0)
    return part


def _paged_unit(kbuf, vbuf, slot, qrep_ref, bias, tri, carry_scr, acc_scr, *, pages):
    rows = N_HEADS * SUBLANES
    zs = [_head_dots(kbuf[slot, p, h], qrep_ref[h]) + bias[h] for p in range(pages) for h in range(N_HEADS)]
    z = jnp.concatenate(zs, axis=0)
    sp = _softplus2(z)
    hi, lo = _split_bf16(sp)
    within = _dot(hi, tri) + _dot(lo, tri)
    tot = jnp.sum(sp, axis=-1, keepdims=True)
    run = carry_scr[:, 0:1]
    carries = [None] * pages
    for p in range(pages - 1, -1, -1):
        carries[p] = run
        run = run + tot[p * rows:(p + 1) * rows]
    carry_scr[...] = jnp.broadcast_to(run, carry_scr.shape)
    w = jnp.exp2(z - within - jnp.concatenate(carries, axis=0))
    for h in range(N_HEADS):
        acc = acc_scr[h]
        for p in range(pages):
            r0 = (p * N_HEADS + h) * SUBLANES
            acc = acc + vbuf[slot, p, h] * jnp.tile(w[r0:r0 + SUBLANES], (HEAD_DIM // SUBLANES, 1))
        acc_scr[h] = acc


def _when(cond):
    if isinstance(cond, bool):
        return (lambda fn: fn()) if cond else (lambda fn: None)
    return pl.when(cond)


def _paged_run_unit(u, n_units, pt_ref, bias_ref, qrep_ref, knrep_ref, vnrep_ref, kc_ref, vc_ref, o_ref,
                    kbuf, vbuf, sems, carry_scr, acc_scr, *, layer, pages, units_per_seq, chunk=None):
    page = kbuf.shape[-1]
    copies = functools.partial(_page_copies, pt_ref, kc_ref, vc_ref, kbuf, vbuf, sems,
                               layer=layer, pages=pages, units_per_seq=units_per_seq)
    slot = lax.rem(u, 2)

    @pl.when(u + 1 < n_units)
    def _():
        for c in copies(u + 1, 1 - slot):
            c.start()

    if chunk is None:
        chunk = lax.rem(u, units_per_seq)
    bias = [bias_ref[h] * LOG2E for h in range(N_HEADS)]
    tri = _suffix_ones(page)

    @_when(chunk == 0)
    def _():
        visible = jnp.full((SUBLANES, page), False)
        first_lane = lax.broadcasted_iota(jnp.int32, (HEAD_DIM, page), 1) == 0
        sp_rows = []
        for h in range(N_HEADS):
            z_new = _head_dots(knrep_ref[h], qrep_ref[h]) + bias[h]
            sp_new = jnp.where(visible, _softplus2(z_new), 0.0)
            w_new = jnp.where(visible, jnp.exp2(z_new - sp_new), 0.0)
            contrib = vnrep_ref[h] * jnp.tile(w_new, (HEAD_DIM // SUBLANES, 1))
            acc_scr[h] = jnp.where(first_lane, contrib, 0.0)
            sp_rows.append(sp_new)
        carry_scr[...] = jnp.concatenate(sp_rows, axis=0)

    for c in copies(u, slot):
        c.wait()
    _paged_unit(kbuf, vbuf, slot, qrep_ref, bias, tri, carry_scr, acc_scr, pages=pages)

    @_when(chunk == units_per_seq - 1)
    def _():
        o_ref[...] = jnp.sum(acc_scr[...].reshape(GROUP_WIDTH, page), axis=-1, keepdims=True)


def _paged_prime(pt_ref, kc_ref, vc_ref, kbuf, vbuf, sems, *, layer, pages, units_per_seq):
    for c in _page_copies(pt_ref, kc_ref, vc_ref, kbuf, vbuf, sems, 0, 0,
                          layer=layer, pages=pages, units_per_seq=units_per_seq):
        c.start()


def _paged_scratch(pages, page):
    return [pltpu.VMEM((2, pages, N_HEADS, HEAD_DIM, page), F32),
            pltpu.VMEM((2, pages, N_HEADS, HEAD_DIM, page), F32),
            pltpu.SemaphoreType.DMA((2, 2)),
            pltpu.VMEM((N_HEADS * SUBLANES, page), F32),
            pltpu.VMEM((N_HEADS, HEAD_DIM, page), F32)]


def _sattn_kernel(pt_ref, bias_ref, qrep_ref, knrep_ref, vnrep_ref, kc_ref, vc_ref, o_ref,
                  kbuf, vbuf, sems, carry_scr, acc_scr, **cfg):
    u = pl.program_id(0)

    @pl.when(u == 0)
    def _():
        _paged_prime(pt_ref, kc_ref, vc_ref, kbuf, vbuf, sems, **cfg)

    _paged_run_unit(u, pl.num_programs(0), pt_ref, bias_ref, qrep_ref, knrep_ref, vnrep_ref, kc_ref, vc_ref,
                    o_ref, kbuf, vbuf, sems, carry_scr, acc_scr, **cfg)


def _sattn(page_table, bias, qrep, knrep, vnrep, cache_k, cache_v, *, layer, pages):
    nb, n_pages = page_table.shape
    page = cache_k.shape[-1]
    units_per_seq = n_pages // pages
    seq_blk = lambda u, pt: (u // units_per_seq, 0, 0, 0)
    rep_spec = pl.BlockSpec((None, N_HEADS, HEAD_DIM, page), seq_blk)
    grid_spec = pltpu.PrefetchScalarGridSpec(
        num_scalar_prefetch=1,
        grid=(nb * units_per_seq,),
        in_specs=[pl.BlockSpec(memory_space=pltpu.SMEM), rep_spec, rep_spec, rep_spec,
                  pl.BlockSpec(memory_space=pl.ANY), pl.BlockSpec(memory_space=pl.ANY)],
        out_specs=pl.BlockSpec((None, GROUP_WIDTH, 1), lambda u, pt: (u // units_per_seq, 0, 0)),
        scratch_shapes=_paged_scratch(pages, page),
    )
    return pl.pallas_call(
        functools.partial(_sattn_kernel, layer=layer, pages=pages, units_per_seq=units_per_seq),
        grid_spec=grid_spec,
        out_shape=jax.ShapeDtypeStruct((nb, GROUP_WIDTH, 1), F32),
        compiler_params=_cparams(1),
        name="paged_sb_attn",
    )(page_table, bias, qrep, knrep, vnrep, cache_k, cache_v)


def _outffn_paged_kernel(pt_ref, x_ref, a_ref, b_ref, c_ref, d_ref, wo_ref, g_ref, w1_ref, w2_ref, gf_ref,
                         bias_ref, qrep_ref, knrep_ref, vnrep_ref, kc_ref, vc_ref, y_ref, o_ref,
                         kbuf, vbuf, sems, carry_scr, acc_scr, *, final_norm, units_per_step, **cfg):
    i = pl.program_id(0)
    n_units = pl.num_programs(0) * units_per_step
    n_chunks = D_FF // FF_CHUNK

    @pl.when(i == 0)
    def _():
        _paged_prime(pt_ref, kc_ref, vc_ref, kbuf, vbuf, sems, **cfg)

    def before_chunk(c):
        for k in range(units_per_step):
            if (k * n_chunks) // units_per_step == c:
                _paged_run_unit(i * units_per_step + k, n_units, pt_ref, bias_ref, qrep_ref, knrep_ref, vnrep_ref,
                                kc_ref, vc_ref, o_ref, kbuf, vbuf, sems, carry_scr, acc_scr, chunk=k, **cfg)

    _outffn_kernel(x_ref, a_ref, b_ref, c_ref, d_ref, wo_ref, g_ref, w1_ref, w2_ref, gf_ref, y_ref,
                   final_norm=final_norm, before_chunk=before_chunk)


def _paged_fusable(n_steps, n_seqs, n_pages, pages):
    del n_pages, pages
    return n_seqs == n_steps


def _outffn_paged(x, a, b, c, d, wo, g, w1, w2, gf, page_table, bias, qrep, knrep, vnrep, cache_k, cache_v,
                  *, layer, tm, final_norm, pages):
    rows = x.shape[0]
    n_steps = rows // tm
    nb, n_pages = page_table.shape
    page = cache_k.shape[-1]
    units_per_seq = n_pages // pages
    units_per_step = nb * units_per_seq // n_steps
    row_blk = lambda i, pt: (i, 0)
    full = pl.BlockSpec((tm, D_MODEL), row_blk)
    gw = pl.BlockSpec((tm, GROUP_WIDTH), row_blk)
    seq_of = lambda i: (i * units_per_step) // units_per_seq
    rep_spec = pl.BlockSpec((None, N_HEADS, HEAD_DIM, page), lambda i, pt: (seq_of(i), 0, 0, 0))
    grid_spec = pltpu.PrefetchScalarGridSpec(
        num_scalar_prefetch=1,
        grid=(n_steps,),
        in_specs=[full, gw, gw, gw, gw, _layer_block(wo, layer), _layer_block(g, layer),
                  _layer_block(w1, layer), _layer_block(w2, layer), _whole(gf),
                  pl.BlockSpec(memory_space=pltpu.SMEM), rep_spec, rep_spec, rep_spec,
                  pl.BlockSpec(memory_space=pl.ANY), pl.BlockSpec(memory_space=pl.ANY)],
        out_specs=[full, pl.BlockSpec((None, GROUP_WIDTH, 1), lambda i, pt: (seq_of(i), 0, 0))],
        scratch_shapes=_paged_scratch(pages, page),
    )
    return pl.pallas_call(
        functools.partial(_outffn_paged_kernel, final_norm=final_norm, units_per_step=units_per_step,
                          layer=layer, pages=pages, units_per_seq=units_per_seq),
        grid_spec=grid_spec,
        out_shape=[jax.ShapeDtypeStruct((rows, D_MODEL), F32), jax.ShapeDtypeStruct((nb, GROUP_WIDTH, 1), F32)],
        compiler_params=_cparams(1),
        name="outffn_paged",
    )(page_table, x, a, b, c, d, wo, g, w1, w2, gf, bias, qrep, knrep, vnrep, cache_k, cache_v)


def _rope_tables(pos):
    inv = ROPE_BASE ** (-jnp.arange(0, HEAD_DIM, 2, dtype=F32) / HEAD_DIM)
    ang = pos.astype(F32)[:, None] * inv[None, :]
    cos, sin, zero = jnp.cos(ang), jnp.sin(ang), jnp.zeros_like(ang)
    tile = lambda a, b: jnp.tile(jnp.concatenate([a, b], axis=-1), (1, N_HEADS))
    return tile(cos, cos), tile(-sin, zero), tile(zero, sin)


def _retention_tables(c):
    lg = jnp.log1p(-jnp.exp2(-5.0 - jnp.arange(N_HEADS, dtype=F32)))
    idx = jnp.arange(c, dtype=F32)
    diff = idx[:, None] - idx[None, :]
    dmask = jnp.where(diff >= 0, jnp.exp(jnp.maximum(diff, 0.0) * lg[:, None, None]), 0.0)
    q_decay = jnp.exp((idx[None, :] + 1.0) * lg[:, None])
    k_decay = jnp.exp((c - 1.0 - idx[None, :]) * lg[:, None])
    chunk_decay = jnp.exp(c * lg)
    lanes = lambda hc: jnp.repeat(hc.T, HEAD_DIM, axis=1)
    return (jnp.transpose(dmask, (1, 0, 2)).reshape(c, N_HEADS * c), lanes(q_decay), lanes(k_decay),
            jnp.repeat(chunk_decay, HEAD_DIM)[None, :])


def kernel(x_prompt, x_sample, cache_k, cache_v, state_ret, state_conv, page_table, w_in, w_out, sgu_w, sgu_b, sgu_ln_g, sgu_ln_b, conv_w, conv_b, conv_ln_g, conv_ln_b, sb_bias, norm_mix, norm_ffn, norm_final, w_ff1, w_ff2):
    bp, lp, _ = x_prompt.shape
    bs, ls, _ = x_sample.shape
    assert ls == 1, "sample group is a single-token decode step"
    depth = w_in.shape[0]
    page_size = cache_k.shape[2]
    past_len = page_table.shape[1] * page_size
    rows_p = bp * lp
    tm_p = min(512, lp)
    tc = min(512, lp)

    w_in_b, w_out_b = w_in.astype(BF16), w_out.astype(BF16)
    w_ff1_b, w_ff2_b = w_ff1.astype(BF16), w_ff2.astype(BF16)
    cache_kt = jnp.transpose(cache_k, (0, 1, 3, 4, 2))
    cache_vt = jnp.transpose(cache_v, (0, 1, 3, 4, 2))

    rope_p = _rope_tables(jnp.arange(lp))
    rope_s = _rope_tables(jnp.full((bs,), past_len))
    shared_p = _retention_tables(CHUNK)
    _, qdec_s, _, _ = _retention_tables(1)
    gam_rows = jnp.tile(qdec_s.reshape(N_HEADS, HEAD_DIM)[:, :1], (bs, 1))

    rows3 = lambda a: a[:, None, :]
    norm_mix3, norm_ffn3 = rows3(norm_mix), rows3(norm_ffn)
    lng3, lnb3 = rows3(sgu_ln_g), rows3(sgu_ln_b)
    cb3, clg3, clb3 = rows3(conv_b), rows3(conv_ln_g), rows3(conv_ln_b)
    gf = norm_final[None, :]
    wsg = jnp.transpose(sgu_w, (0, 2, 1, 3)).reshape(depth, CHUNK, N_HEADS * CHUNK)
    bsg = jnp.repeat(jnp.transpose(sgu_b, (0, 2, 1)), HEAD_DIM, axis=2)
    w0 = rows3(jnp.repeat(sgu_w[:, :, 0, 0], HEAD_DIM, axis=1))
    b0 = rows3(jnp.repeat(sgu_b[:, :, 0], HEAD_DIM, axis=1))
    mix_consts = (wsg, bsg, conv_w, cb3, clg3, clb3)
    state_ret_rows = state_ret.reshape(depth, bs * N_HEADS, HEAD_DIM * HEAD_DIM)

    hp = x_prompt.reshape(rows_p, D_MODEL)
    hs = x_sample.reshape(bs, D_MODEL)
    kv_stack = jnp.zeros((depth, bp, N_HEADS, HEAD_DIM, lp), F32)
    kv_prompt = (kv_stack, kv_stack)
    outs = {n: [] for n in ("ks", "vs", "retp", "rets", "convp", "convs", "sguv")}

    n_pages = page_table.shape[1]
    pages = min(PAGES_PER_STEP, n_pages)
    ride_along = _paged_fusable(rows_p // tm_p, bs, n_pages, pages)
    per_head = lambda t: t.reshape(bs * N_HEADS, HEAD_DIM)
    lanes = lambda t: jnp.broadcast_to(t.reshape(bs, N_HEADS, HEAD_DIM, 1), (bs, N_HEADS, HEAD_DIM, page_size))

    for l in range(depth):
        last = l == depth - 1
        ua_s, vn_s, q_s, k_s, v_s, qr_s, kr_s, vc_s, sg_s, glu_s = _inproj_sample(
            hs, norm_mix3, w_in_b, rope_s, lng3, lnb3, layer=l)
        oa_s, od_s = _smix(ua_s, vn_s, glu_s, state_conv, w0, b0, conv_w, cb3, clg3, clb3, layer=l)
        q_h, k_h, v_h = per_head(qr_s), per_head(kr_s), per_head(vc_s)
        s_new, oc_s = _sret(state_ret_rows, jnp.repeat(q_h, HEAD_DIM, axis=1), jnp.repeat(k_h, HEAD_DIM, axis=1),
                            jnp.tile(v_h, (1, HEAD_DIM)), q_h, k_h, v_h, per_head(sg_s), gam_rows, layer=l)
        paged_args = (page_table, sb_bias[l], lanes(q_s), lanes(k_s), lanes(v_s), cache_kt, cache_vt)

        ua, vn, q, ktf, vtf, ktb, vb, qr, kr, vc, sg, glu = _inproj_prompt(
            hp, norm_mix3, w_in_b, rope_p, lng3, lnb3, kv_prompt, layer=l, batch=bp, seq=lp, tm=tm_p)
        kv_prompt = (ktf, vtf)
        oa, oc, od, ret_bd, conv_tail = _mix(ua, vn, qr, kr, vc, sg, glu, mix_consts, shared_p,
                                             layer=l, batch=bp, seq=lp, tc=tc)
        ob = _attn(q, ktb, vb, sb_bias[l], batch=bp, seq=lp)
        ffn_args = (hp, oa, ob, oc, od, w_out_b, norm_ffn3, w_ff1_b, w_ff2_b, gf)
        if ride_along:
            hp, ob_s = _outffn_paged(*ffn_args, *paged_args, layer=l, tm=tm_p, final_norm=last, pages=pages)
        else:
            hp = _outffn(*ffn_args, layer=l, tm=tm_p, final_norm=last)
            ob_s = _sattn(*paged_args, layer=l, pages=pages)
        outs["retp"].append(jnp.stack(
            [ret_bd[:, h * HEAD_DIM:(h + 1) * HEAD_DIM, h * HEAD_DIM:(h + 1) * HEAD_DIM] for h in range(N_HEADS)],
            axis=1))
        outs["convp"].append(conv_tail[:, CONV_TAIL - (CONV_WIDTH - 1):, :])

        hs = _outffn(hs, oa_s, ob_s.reshape(bs, GROUP_WIDTH), oc_s.reshape(bs, GROUP_WIDTH), od_s,
                     w_out_b, norm_ffn3, w_ff1_b, w_ff2_b, gf, layer=l, tm=bs, final_norm=last)
        outs["ks"].append(k_s.reshape(bs, 1, N_HEADS, HEAD_DIM))
        outs["vs"].append(v_s.reshape(bs, 1, N_HEADS, HEAD_DIM))
        outs["rets"].append(s_new.reshape(bs, N_HEADS, HEAD_DIM, HEAD_DIM))
        outs["convs"].append(jnp.concatenate([state_conv[l][:, 1:, :], glu_s[:, None, :]], axis=1))
        outs["sguv"].append(vn_s[:, None, :])

    st = lambda n: jnp.stack(outs[n])
    kp, vp = (jnp.transpose(t, (0, 1, 4, 2, 3)) for t in kv_prompt)
    return (hp.reshape(bp, lp, D_MODEL), hs.reshape(bs, 1, D_MODEL), kp, vp, st("ks"), st("vs"),
            st("retp"), st("rets"), st("convp"), st("convs"), st("sguv"))
```

```python
import functools
import math

import jax
import jax.numpy as jnp
from jax import lax
from jax.experimental import pallas as pl
from jax.experimental.pallas import tpu as pltpu

F32 = jnp.float32
BF16 = jnp.bfloat16

D_MODEL = 1024
GROUP_WIDTH = 256
N_HEADS = 4
HEAD_DIM = 64
N_SPLITS = 11
CHUNK = 128
CONV_WIDTH = 31
CONV_TAIL = 32
SUBLANES = 8
D_FF = 4 * D_MODEL
FF_CHUNK = 1024
EPS = 1e-6
ROPE_BASE = 10000.0
LOG2E = math.log2(math.e)
QK_SCALE = HEAD_DIM ** -0.5
ATTN_TQ = 256
ATTN_TK = 256
SOFTPLUS2_LINEAR_ABOVE = 30.0 * LOG2E
MASKED_LOGIT = -1e30
VMEM_LIMIT_BYTES = 56 * 1024 * 1024
PAGES_PER_STEP = 16
PAGE_SLOTS = 3


def _cparams(n_axes):
    return pltpu.CompilerParams(dimension_semantics=("arbitrary",) * n_axes,
                                vmem_limit_bytes=VMEM_LIMIT_BYTES)


def _layer_block(arr, layer):
    nd = arr.ndim - 1
    return pl.BlockSpec((None,) + arr.shape[1:], lambda *_: (layer,) + (0,) * nd,
                        pipeline_mode=pl.Buffered(1))


def _whole(arr):
    nd = arr.ndim
    return pl.BlockSpec(arr.shape, lambda *_: (0,) * nd, pipeline_mode=pl.Buffered(1))


def _gelu_tanh(x):
    return x * (0.5 * (1.0 + jnp.tanh(0.7978845608028654 * (x + 0.044715 * (x * x * x)))))


def _sigmoid(x):
    return 1.0 / (1.0 + jnp.exp(-x))


def _standardize(x):
    mu = jnp.mean(x, axis=-1, keepdims=True)
    xc = x - mu
    return xc * lax.rsqrt(jnp.mean(xc * xc, axis=-1, keepdims=True) + EPS)


def _rms_scale(x, g):
    return x * lax.rsqrt(jnp.mean(x * x, axis=-1, keepdims=True) + EPS) * g


def _softplus2(z):
    return jnp.maximum(z, jnp.log2(1.0 + jnp.exp2(jnp.minimum(z, SOFTPLUS2_LINEAR_ABOVE))))


def _head_masks(shape):
    lane = lax.broadcasted_iota(jnp.int32, shape, len(shape) - 1)
    return [(lane >= h * HEAD_DIM) & (lane < (h + 1) * HEAD_DIM) for h in range(N_HEADS)]


def _stack_heads(x, masks):
    zero = jnp.zeros_like(x)
    return jnp.concatenate([jnp.where(m, x, zero) for m in masks], axis=0)


def _split_bf16(x):
    hi = x.astype(BF16)
    lo = (x - hi.astype(F32)).astype(BF16)
    return hi, lo


def _suffix_ones(n):
    return (lax.broadcasted_iota(jnp.int32, (n, n), 0)
            >= lax.broadcasted_iota(jnp.int32, (n, n), 1)).astype(BF16)


def _dot(a, b):
    return jnp.dot(a, b, preferred_element_type=F32)


def _dot_nt(a, b):
    return lax.dot_general(a, b, (((1,), (1,)), ((), ())), preferred_element_type=F32)


def _dot_tn(a, b):
    return lax.dot_general(a, b, (((0,), (0,)), ((), ())), preferred_element_type=F32)


def _inproj_kernel(x_ref, g_ref, w_ref, cos_ref, s1_ref, s2_ref, lng_ref, lnb_ref, *refs, prompt):
    n_out = 12 if prompt else 10
    outs = refs[len(refs) - n_out:]
    xb = _rms_scale(x_ref[...], g_ref[...]).astype(BF16)

    def proj(i):
        return _dot(xb, w_ref[:, i * GROUP_WIDTH:(i + 1) * GROUP_WIDTH])

    cos, s1, s2 = cos_ref[...], s1_ref[...], s2_ref[...]

    def rope(t):
        return (t * cos + pltpu.roll(t, GROUP_WIDTH - HEAD_DIM // 2, 1) * s1
                + pltpu.roll(t, HEAD_DIM // 2, 1) * s2)

    if prompt:
        ua_ref, vn_ref, q_ref, ktf_ref, vtf_ref, ktb_ref, vb_ref, qr_ref, kr_ref, vc_ref, sg_ref, glu_ref = outs
    else:
        ua_ref, vn_ref, q_ref, k_ref, v_ref, qr_ref, kr_ref, vc_ref, sg_ref, glu_ref = outs

    ua_ref[...] = _gelu_tanh(proj(0)).astype(ua_ref.dtype)
    vn = _standardize(_gelu_tanh(proj(1))) * lng_ref[...] + lnb_ref[...]
    vn_ref[...] = vn.astype(vn_ref.dtype)
    q_ref[...] = (proj(2) * (QK_SCALE * LOG2E)).astype(q_ref.dtype)
    k, v = proj(3), proj(4)
    if prompt:
        tm = k.shape[0]
        kt, vt = k.T, v.T
        ktf_ref[...] = kt.reshape(N_HEADS, HEAD_DIM, tm)
        vtf_ref[...] = vt.reshape(N_HEADS, HEAD_DIM, tm)
        ktb = kt.astype(BF16)
        for c in range(tm // ATTN_TK):
            ktb_ref[c] = ktb[:, c * ATTN_TK:(c + 1) * ATTN_TK]
        vb_ref[...] = v.astype(BF16)
    else:
        k_ref[...] = k
        v_ref[...] = v
    qr_ref[...] = rope(proj(5)).astype(qr_ref.dtype)
    kr_ref[...] = (rope(proj(6)) * QK_SCALE).astype(kr_ref.dtype)
    vc_ref[...] = proj(7).astype(vc_ref.dtype)
    g = proj(8)
    sg_ref[...] = (g * _sigmoid(g)).astype(sg_ref.dtype)
    glu_ref[...] = proj(9) * _sigmoid(proj(10))


def _inproj_prompt(x, g, w, rope, lng, lnb, kv_prev, *, layer, batch, seq, tm):
    rows = batch * seq
    nt = seq // tm
    row_blk = lambda i: (i, 0)
    pos_blk = lambda i: (i % nt, 0)
    gw = pl.BlockSpec((tm, GROUP_WIDTH), row_blk)
    pos = pl.BlockSpec((tm, GROUP_WIDTH), pos_blk)
    kv_final = pl.BlockSpec((None, None, N_HEADS, HEAD_DIM, tm), lambda i: (layer, i // nt, 0, 0, i % nt))
    kv_shape = jax.ShapeDtypeStruct(kv_prev[0].shape, F32)
    nb = tm // ATTN_TK
    in_specs = [pl.BlockSpec((tm, D_MODEL), row_blk), _layer_block(g, layer), _layer_block(w, layer),
                pos, pos, pos, _layer_block(lng, layer), _layer_block(lnb, layer),
                pl.BlockSpec(memory_space=pl.ANY), pl.BlockSpec(memory_space=pl.ANY)]
    args = [x, g, w, *rope, lng, lnb, *kv_prev]
    aliases = {len(args) - 2: 3, len(args) - 1: 4}
    out_specs = [gw, gw, gw, kv_final, kv_final,
                 pl.BlockSpec((nb, GROUP_WIDTH, ATTN_TK), lambda i: (i, 0, 0)), gw, gw, gw, gw, gw, gw]
    gws = lambda dt: jax.ShapeDtypeStruct((rows, GROUP_WIDTH), dt)
    out_shape = [gws(F32), gws(BF16), gws(BF16), kv_shape, kv_shape,
                 jax.ShapeDtypeStruct((rows // ATTN_TK, GROUP_WIDTH, ATTN_TK), BF16), gws(BF16),
                 gws(BF16), gws(BF16), gws(BF16), gws(F32), gws(F32)]
    return pl.pallas_call(
        functools.partial(_inproj_kernel, prompt=True),
        grid=(rows // tm,), in_specs=in_specs, out_specs=out_specs, out_shape=out_shape,
        input_output_aliases=aliases, compiler_params=_cparams(1), name="inproj",
    )(*args)


def _inproj_sample(x, g, w, rope, lng, lnb, *, layer):
    rows = x.shape[0]
    blk = lambda i: (0, 0)
    gw = pl.BlockSpec((rows, GROUP_WIDTH), blk)
    return pl.pallas_call(
        functools.partial(_inproj_kernel, prompt=False),
        grid=(1,),
        in_specs=[pl.BlockSpec((rows, D_MODEL), blk), _layer_block(g, layer), _layer_block(w, layer),
                  gw, gw, gw, _layer_block(lng, layer), _layer_block(lnb, layer)],
        out_specs=[gw] * 10,
        out_shape=[jax.ShapeDtypeStruct((rows, GROUP_WIDTH), F32)] * 10,
        compiler_params=_cparams(1), name="inproj_sample",
    )(x, g, w, *rope, lng, lnb)


def _mix_kernel(ua_ref, vn_ref, qr_ref, kr_ref, vc_ref, sg_ref, glu_ref,
                wsg_ref, bsg_ref, dmask_ref, qdec_ref, kdec_ref, cdec_ref,
                cw_ref, cb_ref, clg_ref, clb_ref,
                oa_ref, oc_ref, od_ref, ret_ref, cst_ref,
                s_scr, buf_scr, shift_scr, *, tc, conv_rows):
    j = pl.program_id(1)

    @pl.when(j == 0)
    def _():
        s_scr[...] = jnp.zeros(s_scr.shape, F32)
        buf_scr[0:CONV_TAIL, :] = jnp.zeros((CONV_TAIL, GROUP_WIDTH), F32)

    masks = _head_masks((CHUNK, GROUP_WIDTH))
    row_head = lax.broadcasted_iota(jnp.int32, (GROUP_WIDTH, GROUP_WIDTH), 0) // HEAD_DIM
    col_head = lax.broadcasted_iota(jnp.int32, (GROUP_WIDTH, GROUP_WIDTH), 1) // HEAD_DIM
    blockdiag = row_head == col_head
    t_idx = lax.broadcasted_iota(jnp.int32, (CHUNK, N_HEADS * CHUNK), 0)
    s_idx = lax.broadcasted_iota(jnp.int32, (CHUNK, N_HEADS * CHUNK), 1) % CHUNK
    ws = jnp.where(s_idx <= t_idx, wsg_ref[...], 0.0).astype(BF16)
    dmask = dmask_ref[...]
    qdec, kdec, cdec = qdec_ref[...], kdec_ref[...], cdec_ref[...]

    for c in range(tc // CHUNK):
        rows = pl.ds(c * CHUNK, CHUNK)
        mixed = _dot(ws, _stack_heads(vn_ref[rows, :], masks)) + bsg_ref[...]
        oa_ref[rows, :] = (ua_ref[rows, :] * mixed).astype(oa_ref.dtype)
        qr, kr, vc = qr_ref[rows, :], kr_ref[rows, :], vc_ref[rows, :]
        scores = _dot_nt(qr, _stack_heads(kr, masks)) * dmask
        intra = _dot(scores.astype(BF16), _stack_heads(vc, masks))
        state = s_scr[...]
        inter = _dot(qr, state.astype(BF16)) * qdec
        o = intra + inter
        mu = jnp.zeros_like(o)
        for m in masks:
            mu = mu + jnp.where(m, jnp.sum(jnp.where(m, o, 0.0), axis=-1, keepdims=True), 0.0)
        oc = o - mu * (1.0 / HEAD_DIM)
        var = jnp.zeros_like(o)
        sq = oc * oc
        for m in masks:
            var = var + jnp.where(m, jnp.sum(jnp.where(m, sq, 0.0), axis=-1, keepdims=True), 0.0)
        on = oc * lax.rsqrt(var * (1.0 / HEAD_DIM) + EPS)
        oc_ref[rows, :] = (sg_ref[rows, :] * on).astype(oc_ref.dtype)
        kd = (kr.astype(F32) * kdec).astype(BF16)
        upd = _dot_tn(kd, vc)
        s_scr[...] = state * cdec + jnp.where(blockdiag, upd, 0.0)

    buf_scr[CONV_TAIL:CONV_TAIL + tc, :] = glu_ref[...]
    span = tc + CONV_TAIL - SUBLANES
    for r in range(1, SUBLANES):
        shift_scr[r - 1, 0:span, :] = buf_scr[pl.ds(r, span), :]
    first_tap = CONV_TAIL - (CONV_WIDTH - 1)
    for r0 in range(0, tc, conv_rows):
        acc = jnp.zeros((conv_rows, GROUP_WIDTH), F32) + cb_ref[...]
        for k in range(CONV_WIDTH):
            off = first_tap + k
            base = r0 + (off // SUBLANES) * SUBLANES
            if off % SUBLANES == 0:
                rows_k = buf_scr[pl.ds(base, conv_rows), :]
            else:
                rows_k = shift_scr[off % SUBLANES - 1, pl.ds(base, conv_rows), :]
            acc = acc + cw_ref[k:k + 1, :] * rows_k
        y = _standardize(acc) * clg_ref[...] + clb_ref[...]
        od_ref[pl.ds(r0, conv_rows), :] = (y * _sigmoid(y)).astype(od_ref.dtype)
    tail = buf_scr[tc:tc + CONV_TAIL, :]
    buf_scr[0:CONV_TAIL, :] = tail

    @pl.when(j == pl.num_programs(1) - 1)
    def _():
        ret_ref[0] = s_scr[...]
        cst_ref[0] = tail


def _mix(ua, vn, qr, kr, vc, sg, glu, layer_consts, shared_consts, *, layer, batch, seq, tc):
    nj = seq // tc
    row_blk = lambda b, j: (b * nj + j, 0)
    gw = pl.BlockSpec((tc, GROUP_WIDTH), row_blk)
    wsg, bsg, cw, cb, clg, clb = layer_consts
    dmask, qdec, kdec, cdec = shared_consts
    const_specs = ([_layer_block(wsg, layer), _layer_block(bsg, layer)]
                   + [_whole(c) for c in shared_consts]
                   + [_layer_block(c, layer) for c in (cw, cb, clg, clb)])
    rows = batch * seq
    return pl.pallas_call(
        functools.partial(_mix_kernel, tc=tc, conv_rows=min(64, tc)),
        grid=(batch, nj),
        in_specs=[gw] * 7 + const_specs,
        out_specs=[gw, gw, gw,
                   pl.BlockSpec((1, GROUP_WIDTH, GROUP_WIDTH), lambda b, j: (b, 0, 0)),
                   pl.BlockSpec((1, CONV_TAIL, GROUP_WIDTH), lambda b, j: (b, 0, 0))],
        out_shape=[jax.ShapeDtypeStruct((rows, GROUP_WIDTH), BF16)] * 3
                  + [jax.ShapeDtypeStruct((batch, GROUP_WIDTH, GROUP_WIDTH), F32),
                     jax.ShapeDtypeStruct((batch, CONV_TAIL, GROUP_WIDTH), F32)],
        scratch_shapes=[pltpu.VMEM((GROUP_WIDTH, GROUP_WIDTH), F32),
                        pltpu.VMEM((tc + CONV_TAIL, GROUP_WIDTH), F32),
                        pltpu.VMEM((SUBLANES - 1, tc + CONV_TAIL, GROUP_WIDTH), F32)],
        compiler_params=_cparams(2),
        name="mix",
    )(ua, vn, qr, kr, vc, sg, glu, wsg, bsg, dmask, qdec, kdec, cdec, cw, cb, clg, clb)


def _attn_kernel(bias_ref, q_ref, ktb_ref, vb_ref, o_ref, vbd_scr, s_scr, zc_scr, within_scr, nxt_scr, acc_scr):
    i = pl.program_id(1)
    tq, tk = ATTN_TQ, ATTN_TK
    nkb = ktb_ref.shape[0]

    @pl.when(i == 0)
    def _():
        kmasks = _head_masks((tk, GROUP_WIDTH))

        def build(jb, _):
            vbd_scr[jb] = _stack_heads(vb_ref[pl.ds(pl.multiple_of(jb * tk, tk), tk), :], kmasks)
            return 0

        lax.fori_loop(0, nkb, build, 0)

    q4 = _stack_heads(q_ref[...], _head_masks((tq, GROUP_WIDTH)))
    bias = [bias_ref[h] * LOG2E for h in range(N_HEADS)]
    tri = _suffix_ones(tk)

    def logits(jb):
        s_scr[...] = _dot(q4, ktb_ref[jb])

    def masses(mask):
        s4 = s_scr[...]
        z = jnp.concatenate([s4[h * tq:(h + 1) * tq, :] + bias[h] for h in range(N_HEADS)], axis=0)
        sp = _softplus2(z)
        if mask is not None:
            sp = jnp.where(mask, sp, 0.0)
            z = jnp.where(mask, z, MASKED_LOGIT)
        later = nxt_scr[...]
        zc_scr[...] = z - jnp.concatenate([later] * (tk // CHUNK), axis=1)
        nxt_scr[...] = later + jnp.sum(sp, axis=-1, keepdims=True)
        within_scr[...] = _dot(sp.astype(BF16), tri)

    def outputs(jb):
        w = jnp.exp2(zc_scr[...] - within_scr[...]).astype(BF16)
        wcat = jnp.concatenate([w[h * tq:(h + 1) * tq, :] for h in range(N_HEADS)], axis=1)
        acc_scr[...] += _dot(wcat, vbd_scr[jb])

    acc_scr[...] = jnp.zeros(acc_scr.shape, F32)
    nxt_scr[...] = jnp.zeros(nxt_scr.shape, F32)
    jd = lax.div(i * tq, tk)
    col = lax.broadcasted_iota(jnp.int32, (N_HEADS * tq, tk), 1)
    row = lax.broadcasted_iota(jnp.int32, (N_HEADS * tq, tk), 0) % tq
    logits(jd)
    masses(col < row + (i * tq - jd * tk))
    logits(jnp.maximum(jd - 1, 0))

    def body(t, _):
        outputs(jd - t)
        masses(None)
        logits(jnp.maximum(jd - t - 2, 0))
        return 0

    lax.fori_loop(0, jd, body, 0)
    outputs(0)
    o_ref[...] = acc_scr[...].astype(o_ref.dtype)


def _attn(q, ktb, vb, bias, *, batch, seq):
    nq = seq // ATTN_TQ
    nkb = seq // ATTN_TK
    rows = batch * seq
    return pl.pallas_call(
        _attn_kernel,
        grid=(batch, nq),
        in_specs=[pl.BlockSpec(memory_space=pltpu.SMEM),
                  pl.BlockSpec((ATTN_TQ, GROUP_WIDTH), lambda b, i: (b * nq + i, 0)),
                  pl.BlockSpec((nkb, GROUP_WIDTH, ATTN_TK), lambda b, i: (b, 0, 0)),
                  pl.BlockSpec((seq, GROUP_WIDTH), lambda b, i: (b, 0))],
        out_specs=pl.BlockSpec((ATTN_TQ, GROUP_WIDTH), lambda b, i: (b * nq + i, 0)),
        out_shape=jax.ShapeDtypeStruct((rows, GROUP_WIDTH), BF16),
        scratch_shapes=[pltpu.VMEM((nkb, N_HEADS * ATTN_TK, GROUP_WIDTH), BF16),
                        pltpu.VMEM((N_HEADS * ATTN_TQ, ATTN_TK), F32),
                        pltpu.VMEM((N_HEADS * ATTN_TQ, ATTN_TK), F32),
                        pltpu.VMEM((N_HEADS * ATTN_TQ, ATTN_TK), F32),
                        pltpu.VMEM((N_HEADS * ATTN_TQ, CHUNK), F32),
                        pltpu.VMEM((ATTN_TQ, GROUP_WIDTH), F32)],
        compiler_params=_cparams(2),
        name="sb_attn",
    )(bias, q, ktb, vb)


def _outffn_kernel(x_ref, a_ref, b_ref, c_ref, d_ref, wo_ref, g_ref, w1_ref, w2_ref, gf_ref,
                   y_ref, *, final_norm, before_chunk=None):
    mix = jnp.zeros(x_ref.shape, F32)
    for n, m_ref in enumerate((a_ref, b_ref, c_ref, d_ref)):
        mix = mix + _dot(m_ref[...].astype(BF16), wo_ref[n * GROUP_WIDTH:(n + 1) * GROUP_WIDTH, :])
    h = x_ref[...] + mix
    hn = _rms_scale(h, g_ref[...]).astype(BF16)
    acc = jnp.zeros(x_ref.shape, F32)
    for c in range(D_FF // FF_CHUNK):
        if before_chunk is not None:
            before_chunk(c)
        a = jnp.maximum(_dot(hn, w1_ref[:, c * FF_CHUNK:(c + 1) * FF_CHUNK]), 0.0)
        acc = acc + _dot((a * a).astype(BF16), w2_ref[c * FF_CHUNK:(c + 1) * FF_CHUNK, :])
    y = h + acc
    y_ref[...] = _rms_scale(y, gf_ref[...]) if final_norm else y


def _outffn(x, a, b, c, d, wo, g, w1, w2, gf, *, layer, tm, final_norm):
    rows = x.shape[0]
    row_blk = lambda i: (i, 0)
    full = pl.BlockSpec((tm, D_MODEL), row_blk)
    gw = pl.BlockSpec((tm, GROUP_WIDTH), row_blk)
    return pl.pallas_call(
        functools.partial(_outffn_kernel, final_norm=final_norm),
        grid=(rows // tm,),
        in_specs=[full, gw, gw, gw, gw, _layer_block(wo, layer), _layer_block(g, layer),
                  _layer_block(w1, layer), _layer_block(w2, layer), _whole(gf)],
        out_specs=full,
        out_shape=jax.ShapeDtypeStruct((rows, D_MODEL), F32),
        compiler_params=_cparams(1),
        name="outffn",
    )(x, a, b, c, d, wo, g, w1, w2, gf)


def _smix_kernel(ua_ref, vn_ref, glu_ref, cbuf_ref, w0_ref, b0_ref, cw_ref, cb_ref, clg_ref, clb_ref,
                 oa_ref, od_ref):
    oa_ref[...] = ua_ref[...] * (w0_ref[...] * vn_ref[...] + b0_ref[...])
    hist = cbuf_ref[...] * cw_ref[0:CONV_WIDTH - 1, :][None, :, :]
    y = jnp.sum(hist, axis=1) + cw_ref[CONV_WIDTH - 1:CONV_WIDTH, :] * glu_ref[...] + cb_ref[...]
    y = _standardize(y) * clg_ref[...] + clb_ref[...]
    od_ref[...] = y * _sigmoid(y)


def _smix(ua, vn, glu, cbuf, w0, b0, cw, cb, clg, clb, *, layer):
    n = ua.shape[0]
    gw = pl.BlockSpec((n, GROUP_WIDTH), lambda i: (0, 0))
    return pl.pallas_call(
        _smix_kernel,
        grid=(1,),
        in_specs=[gw, gw, gw] + [_layer_block(c, layer) for c in (cbuf, w0, b0, cw, cb, clg, clb)],
        out_specs=[gw, gw],
        out_shape=[jax.ShapeDtypeStruct((n, GROUP_WIDTH), F32)] * 2,
        compiler_params=_cparams(1),
        name="smix",
    )(ua, vn, glu, cbuf, w0, b0, cw, cb, clg, clb)


def _sret_kernel(s_ref, qrep_ref, krep_ref, vtile_ref, q_ref, k_ref, v_ref, sg_ref, gam_ref,
                 snew_ref, oc_ref):
    s = s_ref[...]
    gam = gam_ref[...]
    snew_ref[...] = s * gam + krep_ref[...] * vtile_ref[...]
    p = qrep_ref[...] * s
    fold = p[:, 0:2 * HEAD_DIM]
    for m in range(1, HEAD_DIM // 2):
        fold = fold + p[:, m * 2 * HEAD_DIM:(m + 1) * 2 * HEAD_DIM]
    inter = fold[:, 0:HEAD_DIM] + fold[:, HEAD_DIM:2 * HEAD_DIM]
    score = jnp.sum(q_ref[...] * k_ref[...], axis=-1, keepdims=True)
    o = score * v_ref[...] + inter * gam
    oc_ref[...] = sg_ref[...] * _standardize(o)


def _sret(s, qrep, krep, vtile, q, k, v, sg, gam, *, layer):
    n = q.shape[0]
    wide = pl.BlockSpec((n, HEAD_DIM * HEAD_DIM), lambda i: (0, 0))
    narrow = pl.BlockSpec((n, HEAD_DIM), lambda i: (0, 0))
    return pl.pallas_call(
        _sret_kernel,
        grid=(1,),
        in_specs=[_layer_block(s, layer), wide, wide, wide, narrow, narrow, narrow, narrow, _whole(gam)],
        out_specs=[wide, narrow],
        out_shape=[jax.ShapeDtypeStruct((n, HEAD_DIM * HEAD_DIM), F32), jax.ShapeDtypeStruct((n, HEAD_DIM), F32)],
        compiler_params=_cparams(1),
        name="sret",
    )(s, qrep, krep, vtile, q, k, v, sg, gam)


def _page_copies(pt_ref, kc_ref, vc_ref, kbuf, vbuf, sems, unit, slot, *, layer, pages, units_per_seq):
    seq = lax.div(unit, units_per_seq)
    first = (units_per_seq - 1 - lax.rem(unit, units_per_seq)) * pages
    copies = []
    for p in range(pages):
        page_id = pt_ref[seq, first + p]
        copies.append(pltpu.make_async_copy(kc_ref.at[layer, page_id], kbuf.at[slot, p], sems.at[0, slot]))
        copies.append(pltpu.make_async_copy(vc_ref.at[layer, page_id], vbuf.at[slot, p], sems.at[1, slot]))
    return copies


def _head_dots(a, b):
    prod = a * b
    part = jnp.sum(prod.reshape(HEAD_DIM // SUBLANES, SUBLANES, prod.shape[-1]), axis=0)
    for shift in (4, 2, 1):
        part = part + pltpu.roll(part, shift, 0)
    return part


def _suffix_sum_lanes(x):
    n = x.shape[-1]
    lane = lax.broadcasted_iota(jnp.int32, x.shape, 1)
    d = 1
    while d < n:
        x = x + jnp.where(lane < n - d, pltpu.roll(x, n - d, 1), 0.0)
        d *= 2
    return x


def _head_bias_rows(bias, n):
    sub = lax.broadcasted_iota(jnp.int32, (SUBLANES, n), 0)
    out = jnp.zeros((SUBLANES, n), F32)
    for h in range(N_HEADS):
        out = jnp.where((sub == h) | (sub == h + N_HEADS), bias[h], out)
    return out


def _paged_unit(kbuf, vbuf, slot, qrep_ref, bias, carry_scr, acc_scr, *, pages):
    page = kbuf.shape[-1]
    per_pack = SUBLANES // N_HEADS
    n_packs = pages // per_pack
    sub = lax.broadcasted_iota(jnp.int32, (SUBLANES, page), 0)
    bias_rows = _head_bias_rows(bias, page)
    zs = []
    for c in range(n_packs):
        z = jnp.zeros((SUBLANES, page), F32)
        for j in range(SUBLANES):
            p, h = c * per_pack + j // N_HEADS, j % N_HEADS
            z = jnp.where(sub == j, _head_dots(kbuf[slot, p, h], qrep_ref[h]), z)
        zs.append(z + bias_rows)
    run = carry_scr[...]
    ws = [None] * n_packs
    for c in range(n_packs - 1, -1, -1):
        sp = _softplus2(zs[c])
        tot = jnp.broadcast_to(jnp.sum(sp, axis=-1, keepdims=True), (SUBLANES, page))
        other = pltpu.roll(tot, N_HEADS, 0)
        after = run + jnp.where(sub < N_HEADS, other, 0.0)
        ws[c] = jnp.exp2(zs[c] - _suffix_sum_lanes(sp) - after)
        run = run + tot + other
    carry_scr[...] = run
    for h in range(N_HEADS):
        acc = acc_scr[h]
        for p in range(pages):
            c, j = p // per_pack, (p % per_pack) * N_HEADS + h
            w_row = jnp.broadcast_to(ws[c][j:j + 1, :], (HEAD_DIM, page))
            acc = acc + vbuf[slot, p, h] * w_row
        acc_scr[h] = acc


def _when(cond):
    if isinstance(cond, bool):
        return (lambda fn: fn()) if cond else (lambda fn: None)
    return pl.when(cond)


def _paged_run_unit(u, n_units, pt_ref, bias_ref, qrep_ref, knrep_ref, vnrep_ref, kc_ref, vc_ref, o_ref,
                    kbuf, vbuf, sems, carry_scr, acc_scr, *, layer, pages, units_per_seq, chunk=None):
    page = kbuf.shape[-1]
    copies = functools.partial(_page_copies, pt_ref, kc_ref, vc_ref, kbuf, vbuf, sems,
                               layer=layer, pages=pages, units_per_seq=units_per_seq)
    slot = lax.rem(u, PAGE_SLOTS)
    ahead = u + (PAGE_SLOTS - 1)

    @pl.when(ahead < n_units)
    def _():
        for c in copies(ahead, lax.rem(ahead, PAGE_SLOTS)):
            c.start()

    if chunk is None:
        chunk = lax.rem(u, units_per_seq)
    bias = [bias_ref[h] * LOG2E for h in range(N_HEADS)]

    @_when(chunk == 0)
    def _():
        visible = jnp.full((SUBLANES, page), False)
        first_lane = lax.broadcasted_iota(jnp.int32, (HEAD_DIM, page), 1) == 0
        sub = lax.broadcasted_iota(jnp.int32, (SUBLANES, page), 0)
        carry = jnp.zeros((SUBLANES, page), F32)
        for h in range(N_HEADS):
            z_new = _head_dots(knrep_ref[h], qrep_ref[h]) + bias[h]
            sp_new = jnp.where(visible, _softplus2(z_new), 0.0)
            w_new = jnp.where(visible, jnp.exp2(z_new - sp_new), 0.0)
            contrib = vnrep_ref[h] * jnp.tile(w_new, (HEAD_DIM // SUBLANES, 1))
            acc_scr[h] = jnp.where(first_lane, contrib, 0.0)
            carry = jnp.where((sub == h) | (sub == h + N_HEADS), sp_new, carry)
        carry_scr[...] = carry

    for c in copies(u, slot):
        c.wait()
    _paged_unit(kbuf, vbuf, slot, qrep_ref, bias, carry_scr, acc_scr, pages=pages)

    @_when(chunk == units_per_seq - 1)
    def _():
        o_ref[...] = jnp.sum(acc_scr[...].reshape(GROUP_WIDTH, page), axis=-1, keepdims=True)


def _paged_prime(n_units, pt_ref, kc_ref, vc_ref, kbuf, vbuf, sems, *, layer, pages, units_per_seq):
    for unit in range(min(PAGE_SLOTS - 1, n_units)):
        for c in _page_copies(pt_ref, kc_ref, vc_ref, kbuf, vbuf, sems, unit, unit,
                              layer=layer, pages=pages, units_per_seq=units_per_seq):
            c.start()


def _paged_scratch(pages, page):
    return [pltpu.VMEM((PAGE_SLOTS, pages, N_HEADS, HEAD_DIM, page), F32),
            pltpu.VMEM((PAGE_SLOTS, pages, N_HEADS, HEAD_DIM, page), F32),
            pltpu.SemaphoreType.DMA((2, PAGE_SLOTS)),
            pltpu.VMEM((SUBLANES, page), F32),
            pltpu.VMEM((N_HEADS, HEAD_DIM, page), F32)]


def _sattn_kernel(pt_ref, bias_ref, qrep_ref, knrep_ref, vnrep_ref, kc_ref, vc_ref, o_ref,
                  kbuf, vbuf, sems, carry_scr, acc_scr, *, n_units, **cfg):
    u = pl.program_id(0)

    @pl.when(u == 0)
    def _():
        _paged_prime(n_units, pt_ref, kc_ref, vc_ref, kbuf, vbuf, sems, **cfg)

    _paged_run_unit(u, n_units, pt_ref, bias_ref, qrep_ref, knrep_ref, vnrep_ref, kc_ref, vc_ref,
                    o_ref, kbuf, vbuf, sems, carry_scr, acc_scr, **cfg)


def _sattn(page_table, bias, qrep, knrep, vnrep, cache_k, cache_v, *, layer, pages):
    nb, n_pages = page_table.shape
    page = cache_k.shape[-1]
    units_per_seq = n_pages // pages
    seq_blk = lambda u, pt: (u // units_per_seq, 0, 0, 0)
    rep_spec = pl.BlockSpec((None, N_HEADS, HEAD_DIM, page), seq_blk)
    grid_spec = pltpu.PrefetchScalarGridSpec(
        num_scalar_prefetch=1,
        grid=(nb * units_per_seq,),
        in_specs=[pl.BlockSpec(memory_space=pltpu.SMEM), rep_spec, rep_spec, rep_spec,
                  pl.BlockSpec(memory_space=pl.ANY), pl.BlockSpec(memory_space=pl.ANY)],
        out_specs=pl.BlockSpec((None, GROUP_WIDTH, 1), lambda u, pt: (u // units_per_seq, 0, 0)),
        scratch_shapes=_paged_scratch(pages, page),
    )
    return pl.pallas_call(
        functools.partial(_sattn_kernel, n_units=nb * units_per_seq,
                          layer=layer, pages=pages, units_per_seq=units_per_seq),
        grid_spec=grid_spec,
        out_shape=jax.ShapeDtypeStruct((nb, GROUP_WIDTH, 1), F32),
        compiler_params=_cparams(1),
        name="paged_sb_attn",
    )(page_table, bias, qrep, knrep, vnrep, cache_k, cache_v)


def _outffn_paged_kernel(pt_ref, x_ref, a_ref, b_ref, c_ref, d_ref, wo_ref, g_ref, w1_ref, w2_ref, gf_ref,
                         bias_ref, qrep_ref, knrep_ref, vnrep_ref, kc_ref, vc_ref, y_ref, o_ref,
                         kbuf, vbuf, sems, carry_scr, acc_scr, *, final_norm, units_per_step, n_units, **cfg):
    i = pl.program_id(0)
    n_chunks = D_FF // FF_CHUNK

    @pl.when(i == 0)
    def _():
        _paged_prime(n_units, pt_ref, kc_ref, vc_ref, kbuf, vbuf, sems, **cfg)

    def before_chunk(c):
        for k in range(units_per_step):
            if (k * n_chunks) // units_per_step == c:
                _paged_run_unit(i * units_per_step + k, n_units, pt_ref, bias_ref, qrep_ref, knrep_ref, vnrep_ref,
                                kc_ref, vc_ref, o_ref, kbuf, vbuf, sems, carry_scr, acc_scr, chunk=k, **cfg)

    _outffn_kernel(x_ref, a_ref, b_ref, c_ref, d_ref, wo_ref, g_ref, w1_ref, w2_ref, gf_ref, y_ref,
                   final_norm=final_norm, before_chunk=before_chunk)


def _paged_fusable(n_steps, n_seqs, n_pages, pages):
    del n_pages, pages
    return n_seqs == n_steps


def _outffn_paged(x, a, b, c, d, wo, g, w1, w2, gf, page_table, bias, qrep, knrep, vnrep, cache_k, cache_v,
                  *, layer, tm, final_norm, pages):
    rows = x.shape[0]
    n_steps = rows // tm
    nb, n_pages = page_table.shape
    page = cache_k.shape[-1]
    units_per_seq = n_pages // pages
    units_per_step = nb * units_per_seq // n_steps
    row_blk = lambda i, pt: (i, 0)
    full = pl.BlockSpec((tm, D_MODEL), row_blk)
    gw = pl.BlockSpec((tm, GROUP_WIDTH), row_blk)
    seq_of = lambda i: (i * units_per_step) // units_per_seq
    rep_spec = pl.BlockSpec((None, N_HEADS, HEAD_DIM, page), lambda i, pt: (seq_of(i), 0, 0, 0))
    grid_spec = pltpu.PrefetchScalarGridSpec(
        num_scalar_prefetch=1,
        grid=(n_steps,),
        in_specs=[full, gw, gw, gw, gw, _layer_block(wo, layer), _layer_block(g, layer),
                  _layer_block(w1, layer), _layer_block(w2, layer), _whole(gf),
                  pl.BlockSpec(memory_space=pltpu.SMEM), rep_spec, rep_spec, rep_spec,
                  pl.BlockSpec(memory_space=pl.ANY), pl.BlockSpec(memory_space=pl.ANY)],
        out_specs=[full, pl.BlockSpec((None, GROUP_WIDTH, 1), lambda i, pt: (seq_of(i), 0, 0))],
        scratch_shapes=_paged_scratch(pages, page),
    )
    return pl.pallas_call(
        functools.partial(_outffn_paged_kernel, final_norm=final_norm, units_per_step=units_per_step,
                          n_units=nb * units_per_seq, layer=layer, pages=pages, units_per_seq=units_per_seq),
        grid_spec=grid_spec,
        out_shape=[jax.ShapeDtypeStruct((rows, D_MODEL), F32), jax.ShapeDtypeStruct((nb, GROUP_WIDTH, 1), F32)],
        compiler_params=_cparams(1),
        name="outffn_paged",
    )(page_table, x, a, b, c, d, wo, g, w1, w2, gf, bias, qrep, knrep, vnrep, cache_k, cache_v)


def _rope_tables(pos):
    inv = ROPE_BASE ** (-jnp.arange(0, HEAD_DIM, 2, dtype=F32) / HEAD_DIM)
    ang = pos.astype(F32)[:, None] * inv[None, :]
    cos, sin, zero = jnp.cos(ang), jnp.sin(ang), jnp.zeros_like(ang)
    tile = lambda a, b: jnp.tile(jnp.concatenate([a, b], axis=-1), (1, N_HEADS))
    return tile(cos, cos), tile(-sin, zero), tile(zero, sin)


def _retention_tables(c):
    lg = jnp.log1p(-jnp.exp2(-5.0 - jnp.arange(N_HEADS, dtype=F32)))
    idx = jnp.arange(c, dtype=F32)
    diff = idx[:, None] - idx[None, :]
    dmask = jnp.where(diff >= 0, jnp.exp(jnp.maximum(diff, 0.0) * lg[:, None, None]), 0.0)
    q_decay = jnp.exp((idx[None, :] + 1.0) * lg[:, None])
    k_decay = jnp.exp((c - 1.0 - idx[None, :]) * lg[:, None])
    chunk_decay = jnp.exp(c * lg)
    lanes = lambda hc: jnp.repeat(hc.T, HEAD_DIM, axis=1)
    return (jnp.transpose(dmask, (1, 0, 2)).reshape(c, N_HEADS * c), lanes(q_decay), lanes(k_decay),
            jnp.repeat(chunk_decay, HEAD_DIM)[None, :])


def kernel(x_prompt, x_sample, cache_k, cache_v, state_ret, state_conv, page_table, w_in, w_out, sgu_w, sgu_b, sgu_ln_g, sgu_ln_b, conv_w, conv_b, conv_ln_g, conv_ln_b, sb_bias, norm_mix, norm_ffn, norm_final, w_ff1, w_ff2):
    bp, lp, _ = x_prompt.shape
    bs, ls, _ = x_sample.shape
    assert ls == 1, "sample group is a single-token decode step"
    depth = w_in.shape[0]
    page_size = cache_k.shape[2]
    past_len = page_table.shape[1] * page_size
    rows_p = bp * lp
    tm_p = min(512, lp)
    tc = min(512, lp)

    w_in_b, w_out_b = w_in.astype(BF16), w_out.astype(BF16)
    w_ff1_b, w_ff2_b = w_ff1.astype(BF16), w_ff2.astype(BF16)
    cache_kt = jnp.transpose(cache_k, (0, 1, 3, 4, 2))
    cache_vt = jnp.transpose(cache_v, (0, 1, 3, 4, 2))

    rope_p = _rope_tables(jnp.arange(lp))
    rope_s = _rope_tables(jnp.full((bs,), past_len))
    shared_p = _retention_tables(CHUNK)
    _, qdec_s, _, _ = _retention_tables(1)
    gam_rows = jnp.tile(qdec_s.reshape(N_HEADS, HEAD_DIM)[:, :1], (bs, 1))

    rows3 = lambda a: a[:, None, :]
    norm_mix3, norm_ffn3 = rows3(norm_mix), rows3(norm_ffn)
    lng3, lnb3 = rows3(sgu_ln_g), rows3(sgu_ln_b)
    cb3, clg3, clb3 = rows3(conv_b), rows3(conv_ln_g), rows3(conv_ln_b)
    gf = norm_final[None, :]
    wsg = jnp.transpose(sgu_w, (0, 2, 1, 3)).reshape(depth, CHUNK, N_HEADS * CHUNK)
    bsg = jnp.repeat(jnp.transpose(sgu_b, (0, 2, 1)), HEAD_DIM, axis=2)
    w0 = rows3(jnp.repeat(sgu_w[:, :, 0, 0], HEAD_DIM, axis=1))
    b0 = rows3(jnp.repeat(sgu_b[:, :, 0], HEAD_DIM, axis=1))
    mix_consts = (wsg, bsg, conv_w, cb3, clg3, clb3)
    state_ret_rows = state_ret.reshape(depth, bs * N_HEADS, HEAD_DIM * HEAD_DIM)

    hp = x_prompt.reshape(rows_p, D_MODEL)
    hs = x_sample.reshape(bs, D_MODEL)
    kv_stack = jnp.zeros((depth, bp, N_HEADS, HEAD_DIM, lp), F32)
    kv_prompt = (kv_stack, kv_stack)
    outs = {n: [] for n in ("ks", "vs", "retp", "rets", "convp", "convs", "sguv")}

    n_pages = page_table.shape[1]
    pages = min(PAGES_PER_STEP, n_pages)
    ride_along = _paged_fusable(rows_p // tm_p, bs, n_pages, pages)
    per_head = lambda t: t.reshape(bs * N_HEADS, HEAD_DIM)
    lanes = lambda t: jnp.broadcast_to(t.reshape(bs, N_HEADS, HEAD_DIM, 1), (bs, N_HEADS, HEAD_DIM, page_size))

    for l in range(depth):
        last = l == depth - 1
        ua_s, vn_s, q_s, k_s, v_s, qr_s, kr_s, vc_s, sg_s, glu_s = _inproj_sample(
            hs, norm_mix3, w_in_b, rope_s, lng3, lnb3, layer=l)
        oa_s, od_s = _smix(ua_s, vn_s, glu_s, state_conv, w0, b0, conv_w, cb3, clg3, clb3, layer=l)
        q_h, k_h, v_h = per_head(qr_s), per_head(kr_s), per_head(vc_s)
        s_new, oc_s = _sret(state_ret_rows, jnp.repeat(q_h, HEAD_DIM, axis=1), jnp.repeat(k_h, HEAD_DIM, axis=1),
                            jnp.tile(v_h, (1, HEAD_DIM)), q_h, k_h, v_h, per_head(sg_s), gam_rows, layer=l)
        paged_args = (page_table, sb_bias[l], lanes(q_s), lanes(k_s), lanes(v_s), cache_kt, cache_vt)

        ua, vn, q, ktf, vtf, ktb, vb, qr, kr, vc, sg, glu = _inproj_prompt(
            hp, norm_mix3, w_in_b, rope_p, lng3, lnb3, kv_prompt, layer=l, batch=bp, seq=lp, tm=tm_p)
        kv_prompt = (ktf, vtf)
        oa, oc, od, ret_bd, conv_tail = _mix(ua, vn, qr, kr, vc, sg, glu, mix_consts, shared_p,
                                             layer=l, batch=bp, seq=lp, tc=tc)
        ob = _attn(q, ktb, vb, sb_bias[l], batch=bp, seq=lp)
        ffn_args = (hp, oa, ob, oc, od, w_out_b, norm_ffn3, w_ff1_b, w_ff2_b, gf)
        if ride_along:
            hp, ob_s = _outffn_paged(*ffn_args, *paged_args, layer=l, tm=tm_p, final_norm=last, pages=pages)
        else:
            hp = _outffn(*ffn_args, layer=l, tm=tm_p, final_norm=last)
            ob_s = _sattn(*paged_args, layer=l, pages=pages)
        outs["retp"].append(jnp.stack(
            [ret_bd[:, h * HEAD_DIM:(h + 1) * HEAD_DIM, h * HEAD_DIM:(h + 1) * HEAD_DIM] for h in range(N_HEADS)],
            axis=1))
        outs["convp"].append(conv_tail[:, CONV_TAIL - (CONV_WIDTH - 1):, :])

        hs = _outffn(hs, oa_s, ob_s.reshape(bs, GROUP_WIDTH), oc_s.reshape(bs, GROUP_WIDTH), od_s,
                     w_out_b, norm_ffn3, w_ff1_b, w_ff2_b, gf, layer=l, tm=bs, final_norm=last)
        outs["ks"].append(k_s.reshape(bs, 1, N_HEADS, HEAD_DIM))
        outs["vs"].append(v_s.reshape(bs, 1, N_HEADS, HEAD_DIM))
        outs["rets"].append(s_new.reshape(bs, N_HEADS, HEAD_DIM, HEAD_DIM))
        outs["convs"].append(jnp.concatenate([state_conv[l][:, 1:, :], glu_s[:, None, :]], axis=1))
        outs["sguv"].append(vn_s[:, None, :])

    st = lambda n: jnp.stack(outs[n])
    kp, vp = (jnp.transpose(t, (0, 1, 4, 2, 3)) for t in kv_prompt)
    return (hp.reshape(bp, lp, D_MODEL), hs.reshape(bs, 1, D_MODEL), kp, vp, st("ks"), st("vs"),
            st("retp"), st("rets"), st("convp"), st("convs"), st("sguv"))
```

```python
import functools
import math

import jax
import jax.numpy as jnp
from jax import lax
from jax.experimental import pallas as pl
from jax.experimental.pallas import tpu as pltpu

F32 = jnp.float32
BF16 = jnp.bfloat16

D_MODEL = 1024
GROUP_WIDTH = 256
N_HEADS = 4
HEAD_DIM = 64
N_SPLITS = 11
CHUNK = 128
CONV_WIDTH = 31
CONV_TAIL = 32
SUBLANES = 8
D_FF = 4 * D_MODEL
FF_CHUNK = 1024
EPS = 1e-6
ROPE_BASE = 10000.0
LOG2E = math.log2(math.e)
QK_SCALE = HEAD_DIM ** -0.5
ATTN_TQ = 256
ATTN_TK = 256
SOFTPLUS2_LINEAR_ABOVE = 30.0 * LOG2E
MASKED_LOGIT = -1e30
VMEM_LIMIT_BYTES = 56 * 1024 * 1024
PAGES_PER_STEP = 16
PAGE_SLOTS = 3


def _cparams(n_axes):
    return pltpu.CompilerParams(dimension_semantics=("arbitrary",) * n_axes,
                                vmem_limit_bytes=VMEM_LIMIT_BYTES)


def _layer_block(arr, layer):
    nd = arr.ndim - 1
    return pl.BlockSpec((None,) + arr.shape[1:], lambda *_: (layer,) + (0,) * nd,
                        pipeline_mode=pl.Buffered(1))


def _whole(arr):
    nd = arr.ndim
    return pl.BlockSpec(arr.shape, lambda *_: (0,) * nd, pipeline_mode=pl.Buffered(1))


def _gelu_tanh(x):
    return x * (0.5 * (1.0 + jnp.tanh(0.7978845608028654 * (x + 0.044715 * (x * x * x)))))


def _sigmoid(x):
    return 1.0 / (1.0 + jnp.exp(-x))


def _standardize(x):
    mu = jnp.mean(x, axis=-1, keepdims=True)
    xc = x - mu
    return xc * lax.rsqrt(jnp.mean(xc * xc, axis=-1, keepdims=True) + EPS)


def _rms_scale(x, g):
    return x * lax.rsqrt(jnp.mean(x * x, axis=-1, keepdims=True) + EPS) * g


def _softplus2(z):
    return jnp.maximum(z, jnp.log2(1.0 + jnp.exp2(jnp.minimum(z, SOFTPLUS2_LINEAR_ABOVE))))


def _head_masks(shape):
    lane = lax.broadcasted_iota(jnp.int32, shape, len(shape) - 1)
    return [(lane >= h * HEAD_DIM) & (lane < (h + 1) * HEAD_DIM) for h in range(N_HEADS)]


def _stack_heads(x, masks):
    zero = jnp.zeros_like(x)
    return jnp.concatenate([jnp.where(m, x, zero) for m in masks], axis=0)


def _split_bf16(x):
    hi = x.astype(BF16)
    lo = (x - hi.astype(F32)).astype(BF16)
    return hi, lo


def _suffix_ones(n):
    return (lax.broadcasted_iota(jnp.int32, (n, n), 0)
            >= lax.broadcasted_iota(jnp.int32, (n, n), 1)).astype(BF16)


def _dot(a, b):
    return jnp.dot(a, b, preferred_element_type=F32)


def _dot_nt(a, b):
    return lax.dot_general(a, b, (((1,), (1,)), ((), ())), preferred_element_type=F32)


def _dot_tn(a, b):
    return lax.dot_general(a, b, (((0,), (0,)), ((), ())), preferred_element_type=F32)


def _inproj_kernel(x_ref, g_ref, w_ref, cos_ref, s1_ref, s2_ref, lng_ref, lnb_ref, *refs, prompt):
    n_out = 12 if prompt else 10
    outs = refs[len(refs) - n_out:]
    xb = _rms_scale(x_ref[...], g_ref[...]).astype(BF16)

    def proj(i):
        return _dot(xb, w_ref[:, i * GROUP_WIDTH:(i + 1) * GROUP_WIDTH])

    cos, s1, s2 = cos_ref[...], s1_ref[...], s2_ref[...]

    def rope(t):
        return (t * cos + pltpu.roll(t, GROUP_WIDTH - HEAD_DIM // 2, 1) * s1
                + pltpu.roll(t, HEAD_DIM // 2, 1) * s2)

    if prompt:
        ua_ref, vn_ref, q_ref, ktf_ref, vtf_ref, ktb_ref, vtb_ref, qr_ref, kr_ref, vc_ref, sg_ref, glu_ref = outs
    else:
        ua_ref, vn_ref, q_ref, k_ref, v_ref, qr_ref, kr_ref, vc_ref, sg_ref, glu_ref = outs

    ua_ref[...] = _gelu_tanh(proj(0)).astype(ua_ref.dtype)
    vn = _standardize(_gelu_tanh(proj(1))) * lng_ref[...] + lnb_ref[...]
    vn_ref[...] = vn.astype(vn_ref.dtype)
    q_ref[...] = (proj(2) * (QK_SCALE * LOG2E)).astype(q_ref.dtype)
    k, v = proj(3), proj(4)
    if prompt:
        tm = k.shape[0]
        kt, vt = k.T, v.T
        ktf_ref[...] = kt.reshape(N_HEADS, HEAD_DIM, tm)
        vtf_ref[...] = vt.reshape(N_HEADS, HEAD_DIM, tm)
        ktb, vtb = kt.astype(BF16), vt.astype(BF16)
        for c in range(tm // ATTN_TK):
            ktb_ref[c] = ktb[:, c * ATTN_TK:(c + 1) * ATTN_TK]
            vtb_ref[c] = vtb[:, c * ATTN_TK:(c + 1) * ATTN_TK]
    else:
        k_ref[...] = k
        v_ref[...] = v
    qr_ref[...] = rope(proj(5)).astype(qr_ref.dtype)
    kr_ref[...] = (rope(proj(6)) * QK_SCALE).astype(kr_ref.dtype)
    vc_ref[...] = proj(7).astype(vc_ref.dtype)
    g = proj(8)
    sg_ref[...] = (g * _sigmoid(g)).astype(sg_ref.dtype)
    glu_ref[...] = proj(9) * _sigmoid(proj(10))


def _inproj_prompt(x, g, w, rope, lng, lnb, kv_prev, *, layer, batch, seq, tm):
    rows = batch * seq
    nt = seq // tm
    row_blk = lambda i: (i, 0)
    pos_blk = lambda i: (i % nt, 0)
    gw = pl.BlockSpec((tm, GROUP_WIDTH), row_blk)
    pos = pl.BlockSpec((tm, GROUP_WIDTH), pos_blk)
    kv_final = pl.BlockSpec((None, None, N_HEADS, HEAD_DIM, tm), lambda i: (layer, i // nt, 0, 0, i % nt))
    kv_shape = jax.ShapeDtypeStruct(kv_prev[0].shape, F32)
    nb = tm // ATTN_TK
    in_specs = [pl.BlockSpec((tm, D_MODEL), row_blk), _layer_block(g, layer), _layer_block(w, layer),
                pos, pos, pos, _layer_block(lng, layer), _layer_block(lnb, layer),
                pl.BlockSpec(memory_space=pl.ANY), pl.BlockSpec(memory_space=pl.ANY)]
    args = [x, g, w, *rope, lng, lnb, *kv_prev]
    aliases = {len(args) - 2: 3, len(args) - 1: 4}
    key_blocks = pl.BlockSpec((nb, GROUP_WIDTH, ATTN_TK), lambda i: (i, 0, 0))
    out_specs = [gw, gw, gw, kv_final, kv_final, key_blocks, key_blocks, gw, gw, gw, gw, gw]
    gws = lambda dt: jax.ShapeDtypeStruct((rows, GROUP_WIDTH), dt)
    blocks_shape = jax.ShapeDtypeStruct((rows // ATTN_TK, GROUP_WIDTH, ATTN_TK), BF16)
    out_shape = [gws(F32), gws(BF16), gws(BF16), kv_shape, kv_shape, blocks_shape, blocks_shape,
                 gws(BF16), gws(BF16), gws(BF16), gws(F32), gws(F32)]
    return pl.pallas_call(
        functools.partial(_inproj_kernel, prompt=True),
        grid=(rows // tm,), in_specs=in_specs, out_specs=out_specs, out_shape=out_shape,
        input_output_aliases=aliases, compiler_params=_cparams(1), name="inproj",
    )(*args)


def _inproj_sample(x, g, w, rope, lng, lnb, *, layer):
    rows = x.shape[0]
    blk = lambda i: (0, 0)
    gw = pl.BlockSpec((rows, GROUP_WIDTH), blk)
    return pl.pallas_call(
        functools.partial(_inproj_kernel, prompt=False),
        grid=(1,),
        in_specs=[pl.BlockSpec((rows, D_MODEL), blk), _layer_block(g, layer), _layer_block(w, layer),
                  gw, gw, gw, _layer_block(lng, layer), _layer_block(lnb, layer)],
        out_specs=[gw] * 10,
        out_shape=[jax.ShapeDtypeStruct((rows, GROUP_WIDTH), F32)] * 10,
        compiler_params=_cparams(1), name="inproj_sample",
    )(x, g, w, *rope, lng, lnb)


def _mix_kernel(ua_ref, vn_ref, qr_ref, kr_ref, vc_ref, sg_ref, glu_ref,
                wsg_ref, bsg_ref, dmask_ref, qdec_ref, kdec_ref, cdec_ref,
                cw_ref, cb_ref, clg_ref, clb_ref,
                oa_ref, oc_ref, od_ref, ret_ref, cst_ref,
                s_scr, buf_scr, shift_scr, *, tc, conv_rows):
    j = pl.program_id(1)

    @pl.when(j == 0)
    def _():
        s_scr[...] = jnp.zeros(s_scr.shape, F32)
        buf_scr[0:CONV_TAIL, :] = jnp.zeros((CONV_TAIL, GROUP_WIDTH), F32)

    masks = _head_masks((CHUNK, GROUP_WIDTH))
    row_head = lax.broadcasted_iota(jnp.int32, (GROUP_WIDTH, GROUP_WIDTH), 0) // HEAD_DIM
    col_head = lax.broadcasted_iota(jnp.int32, (GROUP_WIDTH, GROUP_WIDTH), 1) // HEAD_DIM
    blockdiag = row_head == col_head
    t_idx = lax.broadcasted_iota(jnp.int32, (CHUNK, N_HEADS * CHUNK), 0)
    s_idx = lax.broadcasted_iota(jnp.int32, (CHUNK, N_HEADS * CHUNK), 1) % CHUNK
    ws = jnp.where(s_idx <= t_idx, wsg_ref[...], 0.0).astype(BF16)
    dmask = dmask_ref[...]
    qdec, kdec, cdec = qdec_ref[...], kdec_ref[...], cdec_ref[...]

    for c in range(tc // CHUNK):
        rows = pl.ds(c * CHUNK, CHUNK)
        mixed = _dot(ws, _stack_heads(vn_ref[rows, :], masks)) + bsg_ref[...]
        oa_ref[rows, :] = (ua_ref[rows, :] * mixed).astype(oa_ref.dtype)
        qr, kr, vc = qr_ref[rows, :], kr_ref[rows, :], vc_ref[rows, :]
        scores = _dot_nt(qr, _stack_heads(kr, masks)) * dmask
        intra = _dot(scores.astype(BF16), _stack_heads(vc, masks))
        state = s_scr[...]
        inter = _dot(qr, state.astype(BF16)) * qdec
        o = intra + inter
        mu = jnp.zeros_like(o)
        for m in masks:
            mu = mu + jnp.where(m, jnp.sum(jnp.where(m, o, 0.0), axis=-1, keepdims=True), 0.0)
        oc = o - mu * (1.0 / HEAD_DIM)
        var = jnp.zeros_like(o)
        sq = oc * oc
        for m in masks:
            var = var + jnp.where(m, jnp.sum(jnp.where(m, sq, 0.0), axis=-1, keepdims=True), 0.0)
        on = oc * lax.rsqrt(var * (1.0 / HEAD_DIM) + EPS)
        oc_ref[rows, :] = (sg_ref[rows, :] * on).astype(oc_ref.dtype)
        kd = (kr.astype(F32) * kdec).astype(BF16)
        upd = _dot_tn(kd, vc)
        s_scr[...] = state * cdec + jnp.where(blockdiag, upd, 0.0)

    buf_scr[CONV_TAIL:CONV_TAIL + tc, :] = glu_ref[...]
    span = tc + CONV_TAIL - SUBLANES
    for r in range(1, SUBLANES):
        shift_scr[r - 1, 0:span, :] = buf_scr[pl.ds(r, span), :]
    first_tap = CONV_TAIL - (CONV_WIDTH - 1)
    for r0 in range(0, tc, conv_rows):
        acc = jnp.zeros((conv_rows, GROUP_WIDTH), F32) + cb_ref[...]
        for k in range(CONV_WIDTH):
            off = first_tap + k
            base = r0 + (off // SUBLANES) * SUBLANES
            if off % SUBLANES == 0:
                rows_k = buf_scr[pl.ds(base, conv_rows), :]
            else:
                rows_k = shift_scr[off % SUBLANES - 1, pl.ds(base, conv_rows), :]
            acc = acc + cw_ref[k:k + 1, :] * rows_k
        y = _standardize(acc) * clg_ref[...] + clb_ref[...]
        od_ref[pl.ds(r0, conv_rows), :] = (y * _sigmoid(y)).astype(od_ref.dtype)
    tail = buf_scr[tc:tc + CONV_TAIL, :]
    buf_scr[0:CONV_TAIL, :] = tail

    @pl.when(j == pl.num_programs(1) - 1)
    def _():
        ret_ref[0] = s_scr[...]
        cst_ref[0] = tail


def _mix(ua, vn, qr, kr, vc, sg, glu, layer_consts, shared_consts, *, layer, batch, seq, tc):
    nj = seq // tc
    row_blk = lambda b, j: (b * nj + j, 0)
    gw = pl.BlockSpec((tc, GROUP_WIDTH), row_blk)
    wsg, bsg, cw, cb, clg, clb = layer_consts
    dmask, qdec, kdec, cdec = shared_consts
    const_specs = ([_layer_block(wsg, layer), _layer_block(bsg, layer)]
                   + [_whole(c) for c in shared_consts]
                   + [_layer_block(c, layer) for c in (cw, cb, clg, clb)])
    rows = batch * seq
    return pl.pallas_call(
        functools.partial(_mix_kernel, tc=tc, conv_rows=min(64, tc)),
        grid=(batch, nj),
        in_specs=[gw] * 7 + const_specs,
        out_specs=[gw, gw, gw,
                   pl.BlockSpec((1, GROUP_WIDTH, GROUP_WIDTH), lambda b, j: (b, 0, 0)),
                   pl.BlockSpec((1, CONV_TAIL, GROUP_WIDTH), lambda b, j: (b, 0, 0))],
        out_shape=[jax.ShapeDtypeStruct((rows, GROUP_WIDTH), BF16)] * 3
                  + [jax.ShapeDtypeStruct((batch, GROUP_WIDTH, GROUP_WIDTH), F32),
                     jax.ShapeDtypeStruct((batch, CONV_TAIL, GROUP_WIDTH), F32)],
        scratch_shapes=[pltpu.VMEM((GROUP_WIDTH, GROUP_WIDTH), F32),
                        pltpu.VMEM((tc + CONV_TAIL, GROUP_WIDTH), F32),
                        pltpu.VMEM((SUBLANES - 1, tc + CONV_TAIL, GROUP_WIDTH), F32)],
        compiler_params=_cparams(2),
        name="mix",
    )(ua, vn, qr, kr, vc, sg, glu, wsg, bsg, dmask, qdec, kdec, cdec, cw, cb, clg, clb)


def _attn_kernel(bias_ref, q_ref, ktb_ref, vtb_ref, o_ref, s_scr, zc_scr, within_scr, nxt_scr, acc_scr):
    i = pl.program_id(1)
    tq, tk = ATTN_TQ, ATTN_TK

    q4 = _stack_heads(q_ref[...], _head_masks((tq, GROUP_WIDTH)))
    bias = [bias_ref[h] * LOG2E for h in range(N_HEADS)]
    tri = _suffix_ones(tk)

    def logits(jb):
        s_scr[...] = _dot(q4, ktb_ref[jb])

    def masses(mask):
        s4 = s_scr[...]
        z = jnp.concatenate([s4[h * tq:(h + 1) * tq, :] + bias[h] for h in range(N_HEADS)], axis=0)
        sp = _softplus2(z)
        if mask is not None:
            sp = jnp.where(mask, sp, 0.0)
            z = jnp.where(mask, z, MASKED_LOGIT)
        later = nxt_scr[...]
        zc_scr[...] = z - jnp.concatenate([later] * (tk // CHUNK), axis=1)
        nxt_scr[...] = later + jnp.sum(sp, axis=-1, keepdims=True)
        within_scr[...] = _dot(sp.astype(BF16), tri)

    def outputs(jb):
        w = jnp.exp2(zc_scr[...] - within_scr[...]).astype(BF16)
        vt = vtb_ref[jb]
        for h in range(N_HEADS):
            rows = slice(h * HEAD_DIM, (h + 1) * HEAD_DIM)
            acc_scr[rows, :] += _dot_nt(vt[rows, :], w[h * tq:(h + 1) * tq, :])

    acc_scr[...] = jnp.zeros(acc_scr.shape, F32)
    nxt_scr[...] = jnp.zeros(nxt_scr.shape, F32)
    jd = lax.div(i * tq + (tq - 1), tk)
    col = lax.broadcasted_iota(jnp.int32, (N_HEADS * tq, tk), 1)
    row = lax.broadcasted_iota(jnp.int32, (N_HEADS * tq, tk), 0) % tq
    causal = lambda jb: col < row + (i * tq - jb * tk)

    def step(t, mask):
        outputs(jd - t)
        masses(mask)
        logits(jnp.maximum(jd - t - 2, 0))

    logits(jd)
    masses(causal(jd))
    logits(jnp.maximum(jd - 1, 0))
    for m in range(1, tq // tk):
        step(m - 1, causal(jd - m))

    def body(t, _):
        step(t, None)
        return 0

    lax.fori_loop(tq // tk - 1, jd, body, 0)
    outputs(0)
    o_ref[...] = acc_scr[...].T.astype(o_ref.dtype)


def _attn(q, ktb, vtb, bias, *, batch, seq):
    nq = seq // ATTN_TQ
    nkb = seq // ATTN_TK
    rows = batch * seq
    key_blocks = pl.BlockSpec((nkb, GROUP_WIDTH, ATTN_TK), lambda b, i: (b, 0, 0))
    return pl.pallas_call(
        _attn_kernel,
        grid=(batch, nq),
        in_specs=[pl.BlockSpec(memory_space=pltpu.SMEM),
                  pl.BlockSpec((ATTN_TQ, GROUP_WIDTH), lambda b, i: (b * nq + i, 0)),
                  key_blocks, key_blocks],
        out_specs=pl.BlockSpec((ATTN_TQ, GROUP_WIDTH), lambda b, i: (b * nq + i, 0)),
        out_shape=jax.ShapeDtypeStruct((rows, GROUP_WIDTH), BF16),
        scratch_shapes=[pltpu.VMEM((N_HEADS * ATTN_TQ, ATTN_TK), F32),
                        pltpu.VMEM((N_HEADS * ATTN_TQ, ATTN_TK), F32),
                        pltpu.VMEM((N_HEADS * ATTN_TQ, ATTN_TK), F32),
                        pltpu.VMEM((N_HEADS * ATTN_TQ, CHUNK), F32),
                        pltpu.VMEM((GROUP_WIDTH, ATTN_TQ), F32)],
        compiler_params=_cparams(2),
        name="sb_attn",
    )(bias, q, ktb, vtb)


def _outffn_kernel(x_ref, a_ref, b_ref, c_ref, d_ref, wo_ref, g_ref, w1_ref, w2_ref, gf_ref,
                   y_ref, *, final_norm, before_chunk=None):
    mix = jnp.zeros(x_ref.shape, F32)
    for n, m_ref in enumerate((a_ref, b_ref, c_ref, d_ref)):
        mix = mix + _dot(m_ref[...].astype(BF16), wo_ref[n * GROUP_WIDTH:(n + 1) * GROUP_WIDTH, :])
    h = x_ref[...] + mix
    hn = _rms_scale(h, g_ref[...]).astype(BF16)
    acc = jnp.zeros(x_ref.shape, F32)
    for c in range(D_FF // FF_CHUNK):
        if before_chunk is not None:
            before_chunk(c)
        a = jnp.maximum(_dot(hn, w1_ref[:, c * FF_CHUNK:(c + 1) * FF_CHUNK]), 0.0)
        acc = acc + _dot((a * a).astype(BF16), w2_ref[c * FF_CHUNK:(c + 1) * FF_CHUNK, :])
    y = h + acc
    y_ref[...] = _rms_scale(y, gf_ref[...]) if final_norm else y


def _outffn(x, a, b, c, d, wo, g, w1, w2, gf, *, layer, tm, final_norm):
    rows = x.shape[0]
    row_blk = lambda i: (i, 0)
    full = pl.BlockSpec((tm, D_MODEL), row_blk)
    gw = pl.BlockSpec((tm, GROUP_WIDTH), row_blk)
    return pl.pallas_call(
        functools.partial(_outffn_kernel, final_norm=final_norm),
        grid=(rows // tm,),
        in_specs=[full, gw, gw, gw, gw, _layer_block(wo, layer), _layer_block(g, layer),
                  _layer_block(w1, layer), _layer_block(w2, layer), _whole(gf)],
        out_specs=full,
        out_shape=jax.ShapeDtypeStruct((rows, D_MODEL), F32),
        compiler_params=_cparams(1),
        name="outffn",
    )(x, a, b, c, d, wo, g, w1, w2, gf)


def _smix_kernel(ua_ref, vn_ref, glu_ref, cbuf_ref, w0_ref, b0_ref, cw_ref, cb_ref, clg_ref, clb_ref,
                 oa_ref, od_ref):
    oa_ref[...] = ua_ref[...] * (w0_ref[...] * vn_ref[...] + b0_ref[...])
    hist = cbuf_ref[...] * cw_ref[0:CONV_WIDTH - 1, :][None, :, :]
    y = jnp.sum(hist, axis=1) + cw_ref[CONV_WIDTH - 1:CONV_WIDTH, :] * glu_ref[...] + cb_ref[...]
    y = _standardize(y) * clg_ref[...] + clb_ref[...]
    od_ref[...] = y * _sigmoid(y)


def _smix(ua, vn, glu, cbuf, w0, b0, cw, cb, clg, clb, *, layer):
    n = ua.shape[0]
    gw = pl.BlockSpec((n, GROUP_WIDTH), lambda i: (0, 0))
    return pl.pallas_call(
        _smix_kernel,
        grid=(1,),
        in_specs=[gw, gw, gw] + [_layer_block(c, layer) for c in (cbuf, w0, b0, cw, cb, clg, clb)],
        out_specs=[gw, gw],
        out_shape=[jax.ShapeDtypeStruct((n, GROUP_WIDTH), F32)] * 2,
        compiler_params=_cparams(1),
        name="smix",
    )(ua, vn, glu, cbuf, w0, b0, cw, cb, clg, clb)


def _sret_kernel(s_ref, qrep_ref, krep_ref, vtile_ref, q_ref, k_ref, v_ref, sg_ref, gam_ref,
                 snew_ref, oc_ref):
    s = s_ref[...]
    gam = gam_ref[...]
    snew_ref[...] = s * gam + krep_ref[...] * vtile_ref[...]
    p = qrep_ref[...] * s
    fold = p[:, 0:2 * HEAD_DIM]
    for m in range(1, HEAD_DIM // 2):
        fold = fold + p[:, m * 2 * HEAD_DIM:(m + 1) * 2 * HEAD_DIM]
    inter = fold[:, 0:HEAD_DIM] + fold[:, HEAD_DIM:2 * HEAD_DIM]
    score = jnp.sum(q_ref[...] * k_ref[...], axis=-1, keepdims=True)
    o = score * v_ref[...] + inter * gam
    oc_ref[...] = sg_ref[...] * _standardize(o)


def _sret(s, qrep, krep, vtile, q, k, v, sg, gam, *, layer):
    n = q.shape[0]
    wide = pl.BlockSpec((n, HEAD_DIM * HEAD_DIM), lambda i: (0, 0))
    narrow = pl.BlockSpec((n, HEAD_DIM), lambda i: (0, 0))
    return pl.pallas_call(
        _sret_kernel,
        grid=(1,),
        in_specs=[_layer_block(s, layer), wide, wide, wide, narrow, narrow, narrow, narrow, _whole(gam)],
        out_specs=[wide, narrow],
        out_shape=[jax.ShapeDtypeStruct((n, HEAD_DIM * HEAD_DIM), F32), jax.ShapeDtypeStruct((n, HEAD_DIM), F32)],
        compiler_params=_cparams(1),
        name="sret",
    )(s, qrep, krep, vtile, q, k, v, sg, gam)


def _page_copies(pt_ref, kc_ref, vc_ref, kbuf, vbuf, sems, unit, slot, *, layer, pages, units_per_seq):
    seq = lax.div(unit, units_per_seq)
    first = (units_per_seq - 1 - lax.rem(unit, units_per_seq)) * pages
    copies = []
    for p in range(pages):
        page_id = pt_ref[seq, first + p]
        copies.append(pltpu.make_async_copy(kc_ref.at[layer, page_id], kbuf.at[slot, p], sems.at[0, slot]))
        copies.append(pltpu.make_async_copy(vc_ref.at[layer, page_id], vbuf.at[slot, p], sems.at[1, slot]))
    return copies


def _head_dots(a, b):
    prod = a * b
    part = jnp.sum(prod.reshape(HEAD_DIM // SUBLANES, SUBLANES, prod.shape[-1]), axis=0)
    for shift in (4, 2, 1):
        part = part + pltpu.roll(part, shift, 0)
    return part


def _suffix_sum_lanes(x):
    n = x.shape[-1]
    lane = lax.broadcasted_iota(jnp.int32, x.shape, 1)
    d = 1
    while d < n:
        x = x + jnp.where(lane < n - d, pltpu.roll(x, n - d, 1), 0.0)
        d *= 2
    return x


def _head_bias_rows(bias, n):
    sub = lax.broadcasted_iota(jnp.int32, (SUBLANES, n), 0)
    out = jnp.zeros((SUBLANES, n), F32)
    for h in range(N_HEADS):
        out = jnp.where((sub == h) | (sub == h + N_HEADS), bias[h], out)
    return out


def _paged_unit(kbuf, vbuf, slot, qrep_ref, bias, carry_scr, acc_scr, *, pages):
    page = kbuf.shape[-1]
    per_pack = SUBLANES // N_HEADS
    n_packs = pages // per_pack
    sub = lax.broadcasted_iota(jnp.int32, (SUBLANES, page), 0)
    bias_rows = _head_bias_rows(bias, page)
    zs = []
    for c in range(n_packs):
        z = jnp.zeros((SUBLANES, page), F32)
        for j in range(SUBLANES):
            p, h = c * per_pack + j // N_HEADS, j % N_HEADS
            z = jnp.where(sub == j, _head_dots(kbuf[slot, p, h], qrep_ref[h]), z)
        zs.append(z + bias_rows)
    run = carry_scr[...]
    ws = [None] * n_packs
    for c in range(n_packs - 1, -1, -1):
        sp = _softplus2(zs[c])
        tot = jnp.broadcast_to(jnp.sum(sp, axis=-1, keepdims=True), (SUBLANES, page))
        other = pltpu.roll(tot, N_HEADS, 0)
        after = run + jnp.where(sub < N_HEADS, other, 0.0)
        ws[c] = jnp.exp2(zs[c] - _suffix_sum_lanes(sp) - after)
        run = run + tot + other
    carry_scr[...] = run
    for h in range(N_HEADS):
        acc = acc_scr[h]
        for p in range(pages):
            c, j = p // per_pack, (p % per_pack) * N_HEADS + h
            w_row = jnp.broadcast_to(ws[c][j:j + 1, :], (HEAD_DIM, page))
            acc = acc + vbuf[slot, p, h] * w_row
        acc_scr[h] = acc


def _when(cond):
    if isinstance(cond, bool):
        return (lambda fn: fn()) if cond else (lambda fn: None)
    return pl.when(cond)


def _paged_run_unit(u, n_units, pt_ref, bias_ref, qrep_ref, knrep_ref, vnrep_ref, kc_ref, vc_ref, o_ref,
                    kbuf, vbuf, sems, carry_scr, acc_scr, *, layer, pages, units_per_seq, chunk=None):
    page = kbuf.shape[-1]
    copies = functools.partial(_page_copies, pt_ref, kc_ref, vc_ref, kbuf, vbuf, sems,
                               layer=layer, pages=pages, units_per_seq=units_per_seq)
    slot = lax.rem(u, PAGE_SLOTS)
    ahead = u + (PAGE_SLOTS - 1)

    @pl.when(ahead < n_units)
    def _():
        for c in copies(ahead, lax.rem(ahead, PAGE_SLOTS)):
            c.start()

    if chunk is None:
        chunk = lax.rem(u, units_per_seq)
    bias = [bias_ref[h] * LOG2E for h in range(N_HEADS)]

    @_when(chunk == 0)
    def _():
        visible = jnp.full((SUBLANES, page), False)
        first_lane = lax.broadcasted_iota(jnp.int32, (HEAD_DIM, page), 1) == 0
        sub = lax.broadcasted_iota(jnp.int32, (SUBLANES, page), 0)
        carry = jnp.zeros((SUBLANES, page), F32)
        for h in range(N_HEADS):
            z_new = _head_dots(knrep_ref[h], qrep_ref[h]) + bias[h]
            sp_new = jnp.where(visible, _softplus2(z_new), 0.0)
            w_new = jnp.where(visible, jnp.exp2(z_new - sp_new), 0.0)
            contrib = vnrep_ref[h] * jnp.tile(w_new, (HEAD_DIM // SUBLANES, 1))
            acc_scr[h] = jnp.where(first_lane, contrib, 0.0)
            carry = jnp.where((sub == h) | (sub == h + N_HEADS), sp_new, carry)
        carry_scr[...] = carry

    for c in copies(u, slot):
        c.wait()
    _paged_unit(kbuf, vbuf, slot, qrep_ref, bias, carry_scr, acc_scr, pages=pages)

    @_when(chunk == units_per_seq - 1)
    def _():
        o_ref[...] = jnp.sum(acc_scr[...].reshape(GROUP_WIDTH, page), axis=-1, keepdims=True)


def _paged_prime(n_units, pt_ref, kc_ref, vc_ref, kbuf, vbuf, sems, *, layer, pages, units_per_seq):
    for unit in range(min(PAGE_SLOTS - 1, n_units)):
        for c in _page_copies(pt_ref, kc_ref, vc_ref, kbuf, vbuf, sems, unit, unit,
                              layer=layer, pages=pages, units_per_seq=units_per_seq):
            c.start()


def _paged_scratch(pages, page):
    return [pltpu.VMEM((PAGE_SLOTS, pages, N_HEADS, HEAD_DIM, page), F32),
            pltpu.VMEM((PAGE_SLOTS, pages, N_HEADS, HEAD_DIM, page), F32),
            pltpu.SemaphoreType.DMA((2, PAGE_SLOTS)),
            pltpu.VMEM((SUBLANES, page), F32),
            pltpu.VMEM((N_HEADS, HEAD_DIM, page), F32)]


def _sattn_kernel(pt_ref, bias_ref, qrep_ref, knrep_ref, vnrep_ref, kc_ref, vc_ref, o_ref,
                  kbuf, vbuf, sems, carry_scr, acc_scr, *, n_units, **cfg):
    u = pl.program_id(0)

    @pl.when(u == 0)
    def _():
        _paged_prime(n_units, pt_ref, kc_ref, vc_ref, kbuf, vbuf, sems, **cfg)

    _paged_run_unit(u, n_units, pt_ref, bias_ref, qrep_ref, knrep_ref, vnrep_ref, kc_ref, vc_ref,
                    o_ref, kbuf, vbuf, sems, carry_scr, acc_scr, **cfg)


def _sattn(page_table, bias, qrep, knrep, vnrep, cache_k, cache_v, *, layer, pages):
    nb, n_pages = page_table.shape
    page = cache_k.shape[-1]
    units_per_seq = n_pages // pages
    seq_blk = lambda u, pt: (u // units_per_seq, 0, 0, 0)
    rep_spec = pl.BlockSpec((None, N_HEADS, HEAD_DIM, page), seq_blk)
    grid_spec = pltpu.PrefetchScalarGridSpec(
        num_scalar_prefetch=1,
        grid=(nb * units_per_seq,),
        in_specs=[pl.BlockSpec(memory_space=pltpu.SMEM), rep_spec, rep_spec, rep_spec,
                  pl.BlockSpec(memory_space=pl.ANY), pl.BlockSpec(memory_space=pl.ANY)],
        out_specs=pl.BlockSpec((None, GROUP_WIDTH, 1), lambda u, pt: (u // units_per_seq, 0, 0)),
        scratch_shapes=_paged_scratch(pages, page),
    )
    return pl.pallas_call(
        functools.partial(_sattn_kernel, n_units=nb * units_per_seq,
                          layer=layer, pages=pages, units_per_seq=units_per_seq),
        grid_spec=grid_spec,
        out_shape=jax.ShapeDtypeStruct((nb, GROUP_WIDTH, 1), F32),
        compiler_params=_cparams(1),
        name="paged_sb_attn",
    )(page_table, bias, qrep, knrep, vnrep, cache_k, cache_v)


def _outffn_paged_kernel(pt_ref, x_ref, a_ref, b_ref, c_ref, d_ref, wo_ref, g_ref, w1_ref, w2_ref, gf_ref,
                         bias_ref, qrep_ref, knrep_ref, vnrep_ref, kc_ref, vc_ref, y_ref, o_ref,
                         kbuf, vbuf, sems, carry_scr, acc_scr, *, final_norm, units_per_step, n_units, **cfg):
    i = pl.program_id(0)
    n_chunks = D_FF // FF_CHUNK

    @pl.when(i == 0)
    def _():
        _paged_prime(n_units, pt_ref, kc_ref, vc_ref, kbuf, vbuf, sems, **cfg)

    def before_chunk(c):
        for k in range(units_per_step):
            if (k * n_chunks) // units_per_step == c:
                _paged_run_unit(i * units_per_step + k, n_units, pt_ref, bias_ref, qrep_ref, knrep_ref, vnrep_ref,
                                kc_ref, vc_ref, o_ref, kbuf, vbuf, sems, carry_scr, acc_scr, chunk=k, **cfg)

    _outffn_kernel(x_ref, a_ref, b_ref, c_ref, d_ref, wo_ref, g_ref, w1_ref, w2_ref, gf_ref, y_ref,
                   final_norm=final_norm, before_chunk=before_chunk)


def _paged_fusable(n_steps, n_seqs, n_pages, pages):
    del n_pages, pages
    return n_seqs == n_steps


def _outffn_paged(x, a, b, c, d, wo, g, w1, w2, gf, page_table, bias, qrep, knrep, vnrep, cache_k, cache_v,
                  *, layer, tm, final_norm, pages):
    rows = x.shape[0]
    n_steps = rows // tm
    nb, n_pages = page_table.shape
    page = cache_k.shape[-1]
    units_per_seq = n_pages // pages
    units_per_step = nb * units_per_seq // n_steps
    row_blk = lambda i, pt: (i, 0)
    full = pl.BlockSpec((tm, D_MODEL), row_blk)
    gw = pl.BlockSpec((tm, GROUP_WIDTH), row_blk)
    seq_of = lambda i: (i * units_per_step) // units_per_seq
    rep_spec = pl.BlockSpec((None, N_HEADS, HEAD_DIM, page), lambda i, pt: (seq_of(i), 0, 0, 0))
    grid_spec = pltpu.PrefetchScalarGridSpec(
        num_scalar_prefetch=1,
        grid=(n_steps,),
        in_specs=[full, gw, gw, gw, gw, _layer_block(wo, layer), _layer_block(g, layer),
                  _layer_block(w1, layer), _layer_block(w2, layer), _whole(gf),
                  pl.BlockSpec(memory_space=pltpu.SMEM), rep_spec, rep_spec, rep_spec,
                  pl.BlockSpec(memory_space=pl.ANY), pl.BlockSpec(memory_space=pl.ANY)],
        out_specs=[full, pl.BlockSpec((None, GROUP_WIDTH, 1), lambda i, pt: (seq_of(i), 0, 0))],
        scratch_shapes=_paged_scratch(pages, page),
    )
    return pl.pallas_call(
        functools.partial(_outffn_paged_kernel, final_norm=final_norm, units_per_step=units_per_step,
                          n_units=nb * units_per_seq, layer=layer, pages=pages, units_per_seq=units_per_seq),
        grid_spec=grid_spec,
        out_shape=[jax.ShapeDtypeStruct((rows, D_MODEL), F32), jax.ShapeDtypeStruct((nb, GROUP_WIDTH, 1), F32)],
        compiler_params=_cparams(1),
        name="outffn_paged",
    )(page_table, x, a, b, c, d, wo, g, w1, w2, gf, bias, qrep, knrep, vnrep, cache_k, cache_v)


def _rope_tables(pos):
    inv = ROPE_BASE ** (-jnp.arange(0, HEAD_DIM, 2, dtype=F32) / HEAD_DIM)
    ang = pos.astype(F32)[:, None] * inv[None, :]
    cos, sin, zero = jnp.cos(ang), jnp.sin(ang), jnp.zeros_like(ang)
    tile = lambda a, b: jnp.tile(jnp.concatenate([a, b], axis=-1), (1, N_HEADS))
    return tile(cos, cos), tile(-sin, zero), tile(zero, sin)


def _retention_tables(c):
    lg = jnp.log1p(-jnp.exp2(-5.0 - jnp.arange(N_HEADS, dtype=F32)))
    idx = jnp.arange(c, dtype=F32)
    diff = idx[:, None] - idx[None, :]
    dmask = jnp.where(diff >= 0, jnp.exp(jnp.maximum(diff, 0.0) * lg[:, None, None]), 0.0)
    q_decay = jnp.exp((idx[None, :] + 1.0) * lg[:, None])
    k_decay = jnp.exp((c - 1.0 - idx[None, :]) * lg[:, None])
    chunk_decay = jnp.exp(c * lg)
    lanes = lambda hc: jnp.repeat(hc.T, HEAD_DIM, axis=1)
    return (jnp.transpose(dmask, (1, 0, 2)).reshape(c, N_HEADS * c), lanes(q_decay), lanes(k_decay),
            jnp.repeat(chunk_decay, HEAD_DIM)[None, :])


def kernel(x_prompt, x_sample, cache_k, cache_v, state_ret, state_conv, page_table, w_in, w_out, sgu_w, sgu_b, sgu_ln_g, sgu_ln_b, conv_w, conv_b, conv_ln_g, conv_ln_b, sb_bias, norm_mix, norm_ffn, norm_final, w_ff1, w_ff2):
    bp, lp, _ = x_prompt.shape
    bs, ls, _ = x_sample.shape
    assert ls == 1, "sample group is a single-token decode step"
    depth = w_in.shape[0]
    page_size = cache_k.shape[2]
    past_len = page_table.shape[1] * page_size
    rows_p = bp * lp
    tm_p = min(512, lp)
    tc = min(512, lp)

    w_in_b, w_out_b = w_in.astype(BF16), w_out.astype(BF16)
    w_ff1_b, w_ff2_b = w_ff1.astype(BF16), w_ff2.astype(BF16)
    cache_kt = jnp.transpose(cache_k, (0, 1, 3, 4, 2))
    cache_vt = jnp.transpose(cache_v, (0, 1, 3, 4, 2))

    rope_p = _rope_tables(jnp.arange(lp))
    rope_s = _rope_tables(jnp.full((bs,), past_len))
    shared_p = _retention_tables(CHUNK)
    _, qdec_s, _, _ = _retention_tables(1)
    gam_rows = jnp.tile(qdec_s.reshape(N_HEADS, HEAD_DIM)[:, :1], (bs, 1))

    rows3 = lambda a: a[:, None, :]
    norm_mix3, norm_ffn3 = rows3(norm_mix), rows3(norm_ffn)
    lng3, lnb3 = rows3(sgu_ln_g), rows3(sgu_ln_b)
    cb3, clg3, clb3 = rows3(conv_b), rows3(conv_ln_g), rows3(conv_ln_b)
    gf = norm_final[None, :]
    wsg = jnp.transpose(sgu_w, (0, 2, 1, 3)).reshape(depth, CHUNK, N_HEADS * CHUNK)
    bsg = jnp.repeat(jnp.transpose(sgu_b, (0, 2, 1)), HEAD_DIM, axis=2)
    w0 = rows3(jnp.repeat(sgu_w[:, :, 0, 0], HEAD_DIM, axis=1))
    b0 = rows3(jnp.repeat(sgu_b[:, :, 0], HEAD_DIM, axis=1))
    mix_consts = (wsg, bsg, conv_w, cb3, clg3, clb3)
    state_ret_rows = state_ret.reshape(depth, bs * N_HEADS, HEAD_DIM * HEAD_DIM)

    hp = x_prompt.reshape(rows_p, D_MODEL)
    hs = x_sample.reshape(bs, D_MODEL)
    kv_stack = jnp.zeros((depth, bp, N_HEADS, HEAD_DIM, lp), F32)
    kv_prompt = (kv_stack, kv_stack)
    outs = {n: [] for n in ("ks", "vs", "retp", "rets", "convp", "convs", "sguv")}

    n_pages = page_table.shape[1]
    pages = min(PAGES_PER_STEP, n_pages)
    ride_along = _paged_fusable(rows_p // tm_p, bs, n_pages, pages)
    per_head = lambda t: t.reshape(bs * N_HEADS, HEAD_DIM)
    lanes = lambda t: jnp.broadcast_to(t.reshape(bs, N_HEADS, HEAD_DIM, 1), (bs, N_HEADS, HEAD_DIM, page_size))

    for l in range(depth):
        last = l == depth - 1
        ua_s, vn_s, q_s, k_s, v_s, qr_s, kr_s, vc_s, sg_s, glu_s = _inproj_sample(
            hs, norm_mix3, w_in_b, rope_s, lng3, lnb3, layer=l)
        oa_s, od_s = _smix(ua_s, vn_s, glu_s, state_conv, w0, b0, conv_w, cb3, clg3, clb3, layer=l)
        q_h, k_h, v_h = per_head(qr_s), per_head(kr_s), per_head(vc_s)
        s_new, oc_s = _sret(state_ret_rows, jnp.repeat(q_h, HEAD_DIM, axis=1), jnp.repeat(k_h, HEAD_DIM, axis=1),
                            jnp.tile(v_h, (1, HEAD_DIM)), q_h, k_h, v_h, per_head(sg_s), gam_rows, layer=l)
        paged_args = (page_table, sb_bias[l], lanes(q_s), lanes(k_s), lanes(v_s), cache_kt, cache_vt)

        ua, vn, q, ktf, vtf, ktb, vtb, qr, kr, vc, sg, glu = _inproj_prompt(
            hp, norm_mix3, w_in_b, rope_p, lng3, lnb3, kv_prompt, layer=l, batch=bp, seq=lp, tm=tm_p)
        kv_prompt = (ktf, vtf)
        oa, oc, od, ret_bd, conv_tail = _mix(ua, vn, qr, kr, vc, sg, glu, mix_consts, shared_p,
                                             layer=l, batch=bp, seq=lp, tc=tc)
        ob = _attn(q, ktb, vtb, sb_bias[l], batch=bp, seq=lp)
        ffn_args = (hp, oa, ob, oc, od, w_out_b, norm_ffn3, w_ff1_b, w_ff2_b, gf)
        if ride_along:
            hp, ob_s = _outffn_paged(*ffn_args, *paged_args, layer=l, tm=tm_p, final_norm=last, pages=pages)
        else:
            hp = _outffn(*ffn_args, layer=l, tm=tm_p, final_norm=last)
            ob_s = _sattn(*paged_args, layer=l, pages=pages)
        outs["retp"].append(jnp.stack(
            [ret_bd[:, h * HEAD_DIM:(h + 1) * HEAD_DIM, h * HEAD_DIM:(h + 1) * HEAD_DIM] for h in range(N_HEADS)],
            axis=1))
        outs["convp"].append(conv_tail[:, CONV_TAIL - (CONV_WIDTH - 1):, :])

        hs = _outffn(hs, oa_s, ob_s.reshape(bs, GROUP_WIDTH), oc_s.reshape(bs, GROUP_WIDTH), od_s,
                     w_out_b, norm_ffn3, w_ff1_b, w_ff2_b, gf, layer=l, tm=bs, final_norm=last)
        outs["ks"].append(k_s.reshape(bs, 1, N_HEADS, HEAD_DIM))
        outs["vs"].append(v_s.reshape(bs, 1, N_HEADS, HEAD_DIM))
        outs["rets"].append(s_new.reshape(bs, N_HEADS, HEAD_DIM, HEAD_DIM))
        outs["convs"].append(jnp.concatenate([state_conv[l][:, 1:, :], glu_s[:, None, :]], axis=1))
        outs["sguv"].append(vn_s[:, None, :])

    st = lambda n: jnp.stack(outs[n])
    kp, vp = (jnp.transpose(t, (0, 1, 4, 2, 3)) for t in kv_prompt)
    return (hp.reshape(bp, lp, D_MODEL), hs.reshape(bs, 1, D_MODEL), kp, vp, st("ks"), st("vs"),
            st("retp"), st("rets"), st("convp"), st("convs"), st("sguv"))
```

```python
import functools
import math

import jax
import jax.numpy as jnp
from jax import lax
from jax.experimental import pallas as pl
from jax.experimental.pallas import tpu as pltpu

F32 = jnp.float32
BF16 = jnp.bfloat16

D_MODEL = 1024
GROUP_WIDTH = 256
N_HEADS = 4
HEAD_DIM = 64
N_SPLITS = 11
CHUNK = 128
CONV_WIDTH = 31
CONV_TAIL = 32
SUBLANES = 8
D_FF = 4 * D_MODEL
FF_CHUNK = 1024
EPS = 1e-6
ROPE_BASE = 10000.0
LOG2E = math.log2(math.e)
QK_SCALE = HEAD_DIM ** -0.5
ATTN_TQ = 256
ATTN_TK = 256
SOFTPLUS2_LINEAR_ABOVE = 30.0 * LOG2E
MASKED_LOGIT = -1e30
VMEM_LIMIT_BYTES = 56 * 1024 * 1024
ROW_TILE = 512
CONV_ROWS = 64
PAGES_PER_STEP = 16
PAGE_SLOTS = 3


def _cparams(n_axes):
    return pltpu.CompilerParams(dimension_semantics=("arbitrary",) * n_axes,
                                vmem_limit_bytes=VMEM_LIMIT_BYTES)


def _layer_block(arr, layer):
    nd = arr.ndim - 1
    return pl.BlockSpec((None,) + arr.shape[1:], lambda *_: (layer,) + (0,) * nd,
                        pipeline_mode=pl.Buffered(1))


def _whole(arr):
    nd = arr.ndim
    return pl.BlockSpec(arr.shape, lambda *_: (0,) * nd, pipeline_mode=pl.Buffered(1))


def _gelu_tanh(x):
    return x * (0.5 * (1.0 + jnp.tanh(0.7978845608028654 * (x + 0.044715 * (x * x * x)))))


def _sigmoid(x):
    return 1.0 / (1.0 + jnp.exp(-x))


def _standardize(x):
    mu = jnp.mean(x, axis=-1, keepdims=True)
    xc = x - mu
    return xc * lax.rsqrt(jnp.mean(xc * xc, axis=-1, keepdims=True) + EPS)


def _rms_scale(x, g):
    return x * lax.rsqrt(jnp.mean(x * x, axis=-1, keepdims=True) + EPS) * g


def _softplus2(z):
    return jnp.maximum(z, jnp.log2(1.0 + jnp.exp2(jnp.minimum(z, SOFTPLUS2_LINEAR_ABOVE))))


def _head_masks(shape):
    lane = lax.broadcasted_iota(jnp.int32, shape, len(shape) - 1)
    return [(lane >= h * HEAD_DIM) & (lane < (h + 1) * HEAD_DIM) for h in range(N_HEADS)]


def _stack_heads(x, masks):
    zero = jnp.zeros_like(x)
    return jnp.concatenate([jnp.where(m, x, zero) for m in masks], axis=0)


def _suffix_ones(n):
    return (lax.broadcasted_iota(jnp.int32, (n, n), 0)
            >= lax.broadcasted_iota(jnp.int32, (n, n), 1)).astype(BF16)


def _dot(a, b):
    return jnp.dot(a, b, preferred_element_type=F32)


def _dot_nt(a, b):
    return lax.dot_general(a, b, (((1,), (1,)), ((), ())), preferred_element_type=F32)


def _dot_tn(a, b):
    return lax.dot_general(a, b, (((0,), (0,)), ((), ())), preferred_element_type=F32)


def _inproj_kernel(x_ref, g_ref, w_ref, cos_ref, s1_ref, s2_ref, lng_ref, lnb_ref, *refs, prompt):
    n_out = 12 if prompt else 10
    outs = refs[len(refs) - n_out:]
    xb = _rms_scale(x_ref[...], g_ref[...]).astype(BF16)

    def proj(i):
        return _dot(xb, w_ref[:, i * GROUP_WIDTH:(i + 1) * GROUP_WIDTH])

    cos, s1, s2 = cos_ref[...], s1_ref[...], s2_ref[...]

    def rope(t):
        return (t * cos + pltpu.roll(t, GROUP_WIDTH - HEAD_DIM // 2, 1) * s1
                + pltpu.roll(t, HEAD_DIM // 2, 1) * s2)

    if prompt:
        ua_ref, vn_ref, q_ref, ktf_ref, vtf_ref, ktb_ref, vtb_ref, qr_ref, kr_ref, vc_ref, sg_ref, glu_ref = outs
    else:
        ua_ref, vn_ref, q_ref, k_ref, v_ref, qr_ref, kr_ref, vc_ref, sg_ref, glu_ref = outs

    ua_ref[...] = _gelu_tanh(proj(0)).astype(ua_ref.dtype)
    vn = _standardize(_gelu_tanh(proj(1))) * lng_ref[...] + lnb_ref[...]
    vn_ref[...] = vn.astype(vn_ref.dtype)
    q_ref[...] = (proj(2) * (QK_SCALE * LOG2E)).astype(q_ref.dtype)
    k, v = proj(3), proj(4)
    if prompt:
        tm = k.shape[0]
        kt, vt = k.T, v.T
        ktf_ref[...] = kt.reshape(N_HEADS, HEAD_DIM, tm)
        vtf_ref[...] = vt.reshape(N_HEADS, HEAD_DIM, tm)
        ktb, vtb = kt.astype(BF16), vt.astype(BF16)
        for c in range(tm // ATTN_TK):
            ktb_ref[c] = ktb[:, c * ATTN_TK:(c + 1) * ATTN_TK]
            vtb_ref[c] = vtb[:, c * ATTN_TK:(c + 1) * ATTN_TK]
    else:
        k_ref[...] = k
        v_ref[...] = v
    qr_ref[...] = rope(proj(5)).astype(qr_ref.dtype)
    kr_ref[...] = (rope(proj(6)) * QK_SCALE).astype(kr_ref.dtype)
    vc_ref[...] = proj(7).astype(vc_ref.dtype)
    g = proj(8)
    sg_ref[...] = (g * _sigmoid(g)).astype(sg_ref.dtype)
    glu_ref[...] = proj(9) * _sigmoid(proj(10))


def _inproj_prompt(x, g, w, rope, lng, lnb, kv_prev, *, layer, batch, seq, tm):
    rows = batch * seq
    nt = seq // tm
    row_blk = lambda i: (i, 0)
    pos_blk = lambda i: (i % nt, 0)
    gw = pl.BlockSpec((tm, GROUP_WIDTH), row_blk)
    pos = pl.BlockSpec((tm, GROUP_WIDTH), pos_blk)
    kv_final = pl.BlockSpec((None, None, N_HEADS, HEAD_DIM, tm), lambda i: (layer, i // nt, 0, 0, i % nt))
    kv_shape = jax.ShapeDtypeStruct(kv_prev[0].shape, F32)
    nb = tm // ATTN_TK
    in_specs = [pl.BlockSpec((tm, D_MODEL), row_blk), _layer_block(g, layer), _layer_block(w, layer),
                pos, pos, pos, _layer_block(lng, layer), _layer_block(lnb, layer),
                pl.BlockSpec(memory_space=pl.ANY), pl.BlockSpec(memory_space=pl.ANY)]
    args = [x, g, w, *rope, lng, lnb, *kv_prev]
    aliases = {len(args) - 2: 3, len(args) - 1: 4}
    key_blocks = pl.BlockSpec((nb, GROUP_WIDTH, ATTN_TK), lambda i: (i, 0, 0))
    out_specs = [gw, gw, gw, kv_final, kv_final, key_blocks, key_blocks, gw, gw, gw, gw, gw]
    gws = lambda dt: jax.ShapeDtypeStruct((rows, GROUP_WIDTH), dt)
    blocks_shape = jax.ShapeDtypeStruct((rows // ATTN_TK, GROUP_WIDTH, ATTN_TK), BF16)
    out_shape = [gws(F32), gws(BF16), gws(BF16), kv_shape, kv_shape, blocks_shape, blocks_shape,
                 gws(BF16), gws(BF16), gws(BF16), gws(F32), gws(F32)]
    return pl.pallas_call(
        functools.partial(_inproj_kernel, prompt=True),
        grid=(rows // tm,), in_specs=in_specs, out_specs=out_specs, out_shape=out_shape,
        input_output_aliases=aliases, compiler_params=_cparams(1), name="inproj",
    )(*args)


def _inproj_sample(x, g, w, rope, lng, lnb, *, layer):
    rows = x.shape[0]
    blk = lambda i: (0, 0)
    gw = pl.BlockSpec((rows, GROUP_WIDTH), blk)
    return pl.pallas_call(
        functools.partial(_inproj_kernel, prompt=False),
        grid=(1,),
        in_specs=[pl.BlockSpec((rows, D_MODEL), blk), _layer_block(g, layer), _layer_block(w, layer),
                  gw, gw, gw, _layer_block(lng, layer), _layer_block(lnb, layer)],
        out_specs=[gw] * 10,
        out_shape=[jax.ShapeDtypeStruct((rows, GROUP_WIDTH), F32)] * 10,
        compiler_params=_cparams(1), name="inproj_sample",
    )(x, g, w, *rope, lng, lnb)


def _mix_kernel(ua_ref, vn_ref, qr_ref, kr_ref, vc_ref, sg_ref, glu_ref,
                wsg_ref, bsg_ref, dmask_ref, qdec_ref, kdec_ref, cdec_ref,
                cw_ref, cb_ref, clg_ref, clb_ref,
                oa_ref, oc_ref, od_ref, ret_ref, cst_ref,
                s_scr, buf_scr, shift_scr, *, tc, conv_rows):
    j = pl.program_id(1)

    @pl.when(j == 0)
    def _():
        s_scr[...] = jnp.zeros(s_scr.shape, F32)
        buf_scr[0:CONV_TAIL, :] = jnp.zeros((CONV_TAIL, GROUP_WIDTH), F32)

    masks = _head_masks((CHUNK, GROUP_WIDTH))
    row_head = lax.broadcasted_iota(jnp.int32, (GROUP_WIDTH, GROUP_WIDTH), 0) // HEAD_DIM
    col_head = lax.broadcasted_iota(jnp.int32, (GROUP_WIDTH, GROUP_WIDTH), 1) // HEAD_DIM
    blockdiag = row_head == col_head
    t_idx = lax.broadcasted_iota(jnp.int32, (CHUNK, N_HEADS * CHUNK), 0)
    s_idx = lax.broadcasted_iota(jnp.int32, (CHUNK, N_HEADS * CHUNK), 1) % CHUNK
    ws = jnp.where(s_idx <= t_idx, wsg_ref[...], 0.0).astype(BF16)
    dmask = dmask_ref[...]
    qdec, kdec, cdec = qdec_ref[...], kdec_ref[...], cdec_ref[...]

    for c in range(tc // CHUNK):
        rows = pl.ds(c * CHUNK, CHUNK)
        mixed = _dot(ws, _stack_heads(vn_ref[rows, :], masks)) + bsg_ref[...]
        oa_ref[rows, :] = (ua_ref[rows, :] * mixed).astype(oa_ref.dtype)
        qr, kr, vc = qr_ref[rows, :], kr_ref[rows, :], vc_ref[rows, :]
        scores = _dot_nt(qr, _stack_heads(kr, masks)) * dmask
        intra = _dot(scores.astype(BF16), _stack_heads(vc, masks))
        state = s_scr[...]
        inter = _dot(qr, state.astype(BF16)) * qdec
        o = intra + inter
        mu = jnp.zeros_like(o)
        for m in masks:
            mu = mu + jnp.where(m, jnp.sum(jnp.where(m, o, 0.0), axis=-1, keepdims=True), 0.0)
        oc = o - mu * (1.0 / HEAD_DIM)
        var = jnp.zeros_like(o)
        sq = oc * oc
        for m in masks:
            var = var + jnp.where(m, jnp.sum(jnp.where(m, sq, 0.0), axis=-1, keepdims=True), 0.0)
        on = oc * lax.rsqrt(var * (1.0 / HEAD_DIM) + EPS)
        oc_ref[rows, :] = (sg_ref[rows, :] * on).astype(oc_ref.dtype)
        kd = (kr.astype(F32) * kdec).astype(BF16)
        upd = _dot_tn(kd, vc)
        s_scr[...] = state * cdec + jnp.where(blockdiag, upd, 0.0)

    buf_scr[CONV_TAIL:CONV_TAIL + tc, :] = glu_ref[...]
    span = tc + CONV_TAIL - SUBLANES
    for r in range(1, SUBLANES):
        shift_scr[r - 1, 0:span, :] = buf_scr[pl.ds(r, span), :]
    first_tap = CONV_TAIL - (CONV_WIDTH - 1)
    for r0 in range(0, tc, conv_rows):
        acc = jnp.zeros((conv_rows, GROUP_WIDTH), F32) + cb_ref[...]
        for k in range(CONV_WIDTH):
            off = first_tap + k
            base = r0 + (off // SUBLANES) * SUBLANES
            if off % SUBLANES == 0:
                rows_k = buf_scr[pl.ds(base, conv_rows), :]
            else:
                rows_k = shift_scr[off % SUBLANES - 1, pl.ds(base, conv_rows), :]
            acc = acc + cw_ref[k:k + 1, :] * rows_k
        y = _standardize(acc) * clg_ref[...] + clb_ref[...]
        od_ref[pl.ds(r0, conv_rows), :] = (y * _sigmoid(y)).astype(od_ref.dtype)
    tail = buf_scr[tc:tc + CONV_TAIL, :]
    buf_scr[0:CONV_TAIL, :] = tail

    @pl.when(j == pl.num_programs(1) - 1)
    def _():
        state = s_scr[...]
        for h in range(N_HEADS):
            blk = slice(h * HEAD_DIM, (h + 1) * HEAD_DIM)
            ret_ref[0, h] = state[blk, blk]
        cst_ref[0] = tail[CONV_TAIL - (CONV_WIDTH - 1):, :]


def _mix(ua, vn, qr, kr, vc, sg, glu, layer_consts, shared_consts, *, layer, batch, seq, tc):
    nj = seq // tc
    row_blk = lambda b, j: (b * nj + j, 0)
    gw = pl.BlockSpec((tc, GROUP_WIDTH), row_blk)
    wsg, bsg, cw, cb, clg, clb = layer_consts
    dmask, qdec, kdec, cdec = shared_consts
    const_specs = ([_layer_block(wsg, layer), _layer_block(bsg, layer)]
                   + [_whole(c) for c in shared_consts]
                   + [_layer_block(c, layer) for c in (cw, cb, clg, clb)])
    rows = batch * seq
    return pl.pallas_call(
        functools.partial(_mix_kernel, tc=tc, conv_rows=min(CONV_ROWS, tc)),
        grid=(batch, nj),
        in_specs=[gw] * 7 + const_specs,
        out_specs=[gw, gw, gw,
                   pl.BlockSpec((1, N_HEADS, HEAD_DIM, HEAD_DIM), lambda b, j: (b, 0, 0, 0)),
                   pl.BlockSpec((1, CONV_WIDTH - 1, GROUP_WIDTH), lambda b, j: (b, 0, 0))],
        out_shape=[jax.ShapeDtypeStruct((rows, GROUP_WIDTH), BF16)] * 3
                  + [jax.ShapeDtypeStruct((batch, N_HEADS, HEAD_DIM, HEAD_DIM), F32),
                     jax.ShapeDtypeStruct((batch, CONV_WIDTH - 1, GROUP_WIDTH), F32)],
        scratch_shapes=[pltpu.VMEM((GROUP_WIDTH, GROUP_WIDTH), F32),
                        pltpu.VMEM((tc + CONV_TAIL, GROUP_WIDTH), F32),
                        pltpu.VMEM((SUBLANES - 1, tc + CONV_TAIL, GROUP_WIDTH), F32)],
        compiler_params=_cparams(2),
        name="mix",
    )(ua, vn, qr, kr, vc, sg, glu, wsg, bsg, dmask, qdec, kdec, cdec, cw, cb, clg, clb)


def _attn_kernel(bias_ref, q_ref, ktb_ref, vtb_ref, o_ref, s_scr, zc_scr, within_scr, nxt_scr, acc_scr):
    i = pl.program_id(1)
    tq, tk = ATTN_TQ, ATTN_TK

    q4 = _stack_heads(q_ref[...], _head_masks((tq, GROUP_WIDTH)))
    bias = [bias_ref[h] * LOG2E for h in range(N_HEADS)]
    tri = _suffix_ones(tk)

    def logits(jb):
        s_scr[...] = _dot(q4, ktb_ref[jb])

    def masses(mask):
        s4 = s_scr[...]
        z = jnp.concatenate([s4[h * tq:(h + 1) * tq, :] + bias[h] for h in range(N_HEADS)], axis=0)
        sp = _softplus2(z)
        if mask is not None:
            sp = jnp.where(mask, sp, 0.0)
            z = jnp.where(mask, z, MASKED_LOGIT)
        later = nxt_scr[...]
        zc_scr[...] = z - jnp.concatenate([later] * (tk // CHUNK), axis=1)
        nxt_scr[...] = later + jnp.sum(sp, axis=-1, keepdims=True)
        within_scr[...] = _dot(sp.astype(BF16), tri)

    def outputs(jb):
        w = jnp.exp2(zc_scr[...] - within_scr[...]).astype(BF16)
        vt = vtb_ref[jb]
        for h in range(N_HEADS):
            rows = slice(h * HEAD_DIM, (h + 1) * HEAD_DIM)
            acc_scr[rows, :] += _dot_nt(vt[rows, :], w[h * tq:(h + 1) * tq, :])

    acc_scr[...] = jnp.zeros(acc_scr.shape, F32)
    nxt_scr[...] = jnp.zeros(nxt_scr.shape, F32)
    jd = lax.div(i * tq + (tq - 1), tk)
    col = lax.broadcasted_iota(jnp.int32, (N_HEADS * tq, tk), 1)
    row = lax.broadcasted_iota(jnp.int32, (N_HEADS * tq, tk), 0) % tq
    causal = lambda jb: col < row + (i * tq - jb * tk)

    def step(t, mask):
        outputs(jd - t)
        masses(mask)
        logits(jnp.maximum(jd - t - 2, 0))

    logits(jd)
    masses(causal(jd))
    logits(jnp.maximum(jd - 1, 0))
    for m in range(1, tq // tk):
        step(m - 1, causal(jd - m))

    def body(t, _):
        step(t, None)
        return 0

    lax.fori_loop(tq // tk - 1, jd, body, 0)
    outputs(0)
    o_ref[...] = acc_scr[...].T.astype(o_ref.dtype)


def _attn(q, ktb, vtb, bias, *, batch, seq):
    nq = seq // ATTN_TQ
    nkb = seq // ATTN_TK
    rows = batch * seq
    key_blocks = pl.BlockSpec((nkb, GROUP_WIDTH, ATTN_TK), lambda b, i: (b, 0, 0))
    return pl.pallas_call(
        _attn_kernel,
        grid=(batch, nq),
        in_specs=[pl.BlockSpec(memory_space=pltpu.SMEM),
                  pl.BlockSpec((ATTN_TQ, GROUP_WIDTH), lambda b, i: (b * nq + i, 0)),
                  key_blocks, key_blocks],
        out_specs=pl.BlockSpec((ATTN_TQ, GROUP_WIDTH), lambda b, i: (b * nq + i, 0)),
        out_shape=jax.ShapeDtypeStruct((rows, GROUP_WIDTH), BF16),
        scratch_shapes=[pltpu.VMEM((N_HEADS * ATTN_TQ, ATTN_TK), F32),
                        pltpu.VMEM((N_HEADS * ATTN_TQ, ATTN_TK), F32),
                        pltpu.VMEM((N_HEADS * ATTN_TQ, ATTN_TK), F32),
                        pltpu.VMEM((N_HEADS * ATTN_TQ, CHUNK), F32),
                        pltpu.VMEM((GROUP_WIDTH, ATTN_TQ), F32)],
        compiler_params=_cparams(2),
        name="sb_attn",
    )(bias, q, ktb, vtb)


def _outffn_kernel(x_ref, a_ref, b_ref, c_ref, d_ref, wo_ref, g_ref, w1_ref, w2_ref, gf_ref,
                   y_ref, *, final_norm, before_chunk=None):
    mix = jnp.zeros(x_ref.shape, F32)
    for n, m_ref in enumerate((a_ref, b_ref, c_ref, d_ref)):
        mix = mix + _dot(m_ref[...].astype(BF16), wo_ref[n * GROUP_WIDTH:(n + 1) * GROUP_WIDTH, :])
    h = x_ref[...] + mix
    hn = _rms_scale(h, g_ref[...]).astype(BF16)
    acc = jnp.zeros(x_ref.shape, F32)
    for c in range(D_FF // FF_CHUNK):
        if before_chunk is not None:
            before_chunk(c)
        a = jnp.maximum(_dot(hn, w1_ref[:, c * FF_CHUNK:(c + 1) * FF_CHUNK]), 0.0)
        acc = acc + _dot((a * a).astype(BF16), w2_ref[c * FF_CHUNK:(c + 1) * FF_CHUNK, :])
    y = h + acc
    y_ref[...] = _rms_scale(y, gf_ref[...]) if final_norm else y


def _outffn(x, a, b, c, d, wo, g, w1, w2, gf, *, layer, tm, final_norm):
    rows = x.shape[0]
    row_blk = lambda i: (i, 0)
    full = pl.BlockSpec((tm, D_MODEL), row_blk)
    gw = pl.BlockSpec((tm, GROUP_WIDTH), row_blk)
    return pl.pallas_call(
        functools.partial(_outffn_kernel, final_norm=final_norm),
        grid=(rows // tm,),
        in_specs=[full, gw, gw, gw, gw, _layer_block(wo, layer), _layer_block(g, layer),
                  _layer_block(w1, layer), _layer_block(w2, layer), _whole(gf)],
        out_specs=full,
        out_shape=jax.ShapeDtypeStruct((rows, D_MODEL), F32),
        compiler_params=_cparams(1),
        name="outffn",
    )(x, a, b, c, d, wo, g, w1, w2, gf)


def _smix_kernel(ua_ref, vn_ref, glu_ref, cbuf_ref, w0_ref, b0_ref, cw_ref, cb_ref, clg_ref, clb_ref,
                 oa_ref, od_ref):
    oa_ref[...] = ua_ref[...] * (w0_ref[...] * vn_ref[...] + b0_ref[...])
    hist = cbuf_ref[...] * cw_ref[0:CONV_WIDTH - 1, :][None, :, :]
    y = jnp.sum(hist, axis=1) + cw_ref[CONV_WIDTH - 1:CONV_WIDTH, :] * glu_ref[...] + cb_ref[...]
    y = _standardize(y) * clg_ref[...] + clb_ref[...]
    od_ref[...] = y * _sigmoid(y)


def _smix(ua, vn, glu, cbuf, w0, b0, cw, cb, clg, clb, *, layer):
    n = ua.shape[0]
    gw = pl.BlockSpec((n, GROUP_WIDTH), lambda i: (0, 0))
    return pl.pallas_call(
        _smix_kernel,
        grid=(1,),
        in_specs=[gw, gw, gw] + [_layer_block(c, layer) for c in (cbuf, w0, b0, cw, cb, clg, clb)],
        out_specs=[gw, gw],
        out_shape=[jax.ShapeDtypeStruct((n, GROUP_WIDTH), F32)] * 2,
        compiler_params=_cparams(1),
        name="smix",
    )(ua, vn, glu, cbuf, w0, b0, cw, cb, clg, clb)


def _sret_kernel(s_ref, qrep_ref, krep_ref, vtile_ref, q_ref, k_ref, v_ref, sg_ref, gam_ref,
                 snew_ref, oc_ref):
    s = s_ref[...]
    gam = gam_ref[...]
    snew_ref[...] = s * gam + krep_ref[...] * vtile_ref[...]
    p = qrep_ref[...] * s
    fold = p[:, 0:2 * HEAD_DIM]
    for m in range(1, HEAD_DIM // 2):
        fold = fold + p[:, m * 2 * HEAD_DIM:(m + 1) * 2 * HEAD_DIM]
    inter = fold[:, 0:HEAD_DIM] + fold[:, HEAD_DIM:2 * HEAD_DIM]
    score = jnp.sum(q_ref[...] * k_ref[...], axis=-1, keepdims=True)
    o = score * v_ref[...] + inter * gam
    oc_ref[...] = sg_ref[...] * _standardize(o)


def _sret(s, qrep, krep, vtile, q, k, v, sg, gam, *, layer):
    n = q.shape[0]
    wide = pl.BlockSpec((n, HEAD_DIM * HEAD_DIM), lambda i: (0, 0))
    narrow = pl.BlockSpec((n, HEAD_DIM), lambda i: (0, 0))
    return pl.pallas_call(
        _sret_kernel,
        grid=(1,),
        in_specs=[_layer_block(s, layer), wide, wide, wide, narrow, narrow, narrow, narrow, _whole(gam)],
        out_specs=[wide, narrow],
        out_shape=[jax.ShapeDtypeStruct((n, HEAD_DIM * HEAD_DIM), F32), jax.ShapeDtypeStruct((n, HEAD_DIM), F32)],
        compiler_params=_cparams(1),
        name="sret",
    )(s, qrep, krep, vtile, q, k, v, sg, gam)


def _page_copies(pt_ref, kc_ref, vc_ref, kbuf, vbuf, sems, unit, slot, *, layer, pages, units_per_seq):
    seq = lax.div(unit, units_per_seq)
    first = (units_per_seq - 1 - lax.rem(unit, units_per_seq)) * pages
    copies = []
    for p in range(pages):
        page_id = pt_ref[seq, first + p]
        copies.append(pltpu.make_async_copy(kc_ref.at[layer, page_id], kbuf.at[slot, p], sems.at[0, slot]))
        copies.append(pltpu.make_async_copy(vc_ref.at[layer, page_id], vbuf.at[slot, p], sems.at[1, slot]))
    return copies


def _head_dots(a, b):
    prod = a * b
    part = jnp.sum(prod.reshape(HEAD_DIM // SUBLANES, SUBLANES, prod.shape[-1]), axis=0)
    for shift in (4, 2, 1):
        part = part + pltpu.roll(part, shift, 0)
    return part


def _suffix_sum_lanes(x):
    n = x.shape[-1]
    lane = lax.broadcasted_iota(jnp.int32, x.shape, 1)
    d = 1
    while d < n:
        x = x + jnp.where(lane < n - d, pltpu.roll(x, n - d, 1), 0.0)
        d *= 2
    return x


def _head_bias_rows(bias, n):
    sub = lax.broadcasted_iota(jnp.int32, (SUBLANES, n), 0)
    out = jnp.zeros((SUBLANES, n), F32)
    for h in range(N_HEADS):
        out = jnp.where((sub == h) | (sub == h + N_HEADS), bias[h], out)
    return out


def _paged_unit(kbuf, vbuf, slot, qrep_ref, bias, carry_scr, acc_scr, *, pages):
    page = kbuf.shape[-1]
    per_pack = SUBLANES // N_HEADS
    n_packs = pages // per_pack
    sub = lax.broadcasted_iota(jnp.int32, (SUBLANES, page), 0)
    bias_rows = _head_bias_rows(bias, page)
    zs = []
    for c in range(n_packs):
        z = jnp.zeros((SUBLANES, page), F32)
        for j in range(SUBLANES):
            p, h = c * per_pack + j // N_HEADS, j % N_HEADS
            z = jnp.where(sub == j, _head_dots(kbuf[slot, p, h], qrep_ref[h]), z)
        zs.append(z + bias_rows)
    run = carry_scr[...]
    ws = [None] * n_packs
    for c in range(n_packs - 1, -1, -1):
        sp = _softplus2(zs[c])
        tot = jnp.broadcast_to(jnp.sum(sp, axis=-1, keepdims=True), (SUBLANES, page))
        other = pltpu.roll(tot, N_HEADS, 0)
        after = run + jnp.where(sub < N_HEADS, other, 0.0)
        ws[c] = jnp.exp2(zs[c] - _suffix_sum_lanes(sp) - after)
        run = run + tot + other
    carry_scr[...] = run
    for h in range(N_HEADS):
        acc = acc_scr[h]
        for p in range(pages):
            c, j = p // per_pack, (p % per_pack) * N_HEADS + h
            w_row = jnp.broadcast_to(ws[c][j:j + 1, :], (HEAD_DIM, page))
            acc = acc + vbuf[slot, p, h] * w_row
        acc_scr[h] = acc


def _when(cond):
    if isinstance(cond, bool):
        return (lambda fn: fn()) if cond else (lambda fn: None)
    return pl.when(cond)


def _paged_run_unit(u, n_units, pt_ref, bias_ref, qrep_ref, knrep_ref, vnrep_ref, kc_ref, vc_ref, o_ref,
                    kbuf, vbuf, sems, carry_scr, acc_scr, *, layer, pages, units_per_seq, chunk=None):
    page = kbuf.shape[-1]
    copies = functools.partial(_page_copies, pt_ref, kc_ref, vc_ref, kbuf, vbuf, sems,
                               layer=layer, pages=pages, units_per_seq=units_per_seq)
    slot = lax.rem(u, PAGE_SLOTS)
    ahead = u + (PAGE_SLOTS - 1)

    @pl.when(ahead < n_units)
    def _():
        for c in copies(ahead, lax.rem(ahead, PAGE_SLOTS)):
            c.start()

    if chunk is None:
        chunk = lax.rem(u, units_per_seq)
    bias = [bias_ref[h] * LOG2E for h in range(N_HEADS)]

    @_when(chunk == 0)
    def _():
        visible = jnp.full((SUBLANES, page), False)
        first_lane = lax.broadcasted_iota(jnp.int32, (HEAD_DIM, page), 1) == 0
        sub = lax.broadcasted_iota(jnp.int32, (SUBLANES, page), 0)
        carry = jnp.zeros((SUBLANES, page), F32)
        for h in range(N_HEADS):
            z_new = _head_dots(knrep_ref[h], qrep_ref[h]) + bias[h]
            sp_new = jnp.where(visible, _softplus2(z_new), 0.0)
            w_new = jnp.where(visible, jnp.exp2(z_new - sp_new), 0.0)
            contrib = vnrep_ref[h] * jnp.tile(w_new, (HEAD_DIM // SUBLANES, 1))
            acc_scr[h] = jnp.where(first_lane, contrib, 0.0)
            carry = jnp.where((sub == h) | (sub == h + N_HEADS), sp_new, carry)
        carry_scr[...] = carry

    for c in copies(u, slot):
        c.wait()
    _paged_unit(kbuf, vbuf, slot, qrep_ref, bias, carry_scr, acc_scr, pages=pages)

    @_when(chunk == units_per_seq - 1)
    def _():
        o_ref[...] = jnp.sum(acc_scr[...].reshape(GROUP_WIDTH, page), axis=-1, keepdims=True)


def _paged_prime(n_units, pt_ref, kc_ref, vc_ref, kbuf, vbuf, sems, *, layer, pages, units_per_seq):
    for unit in range(min(PAGE_SLOTS - 1, n_units)):
        for c in _page_copies(pt_ref, kc_ref, vc_ref, kbuf, vbuf, sems, unit, unit,
                              layer=layer, pages=pages, units_per_seq=units_per_seq):
            c.start()


def _paged_scratch(pages, page):
    return [pltpu.VMEM((PAGE_SLOTS, pages, N_HEADS, HEAD_DIM, page), F32),
            pltpu.VMEM((PAGE_SLOTS, pages, N_HEADS, HEAD_DIM, page), F32),
            pltpu.SemaphoreType.DMA((2, PAGE_SLOTS)),
            pltpu.VMEM((SUBLANES, page), F32),
            pltpu.VMEM((N_HEADS, HEAD_DIM, page), F32)]


def _sattn_kernel(pt_ref, bias_ref, qrep_ref, knrep_ref, vnrep_ref, kc_ref, vc_ref, o_ref,
                  kbuf, vbuf, sems, carry_scr, acc_scr, *, n_units, **cfg):
    u = pl.program_id(0)

    @pl.when(u == 0)
    def _():
        _paged_prime(n_units, pt_ref, kc_ref, vc_ref, kbuf, vbuf, sems, **cfg)

    _paged_run_unit(u, n_units, pt_ref, bias_ref, qrep_ref, knrep_ref, vnrep_ref, kc_ref, vc_ref,
                    o_ref, kbuf, vbuf, sems, carry_scr, acc_scr, **cfg)


def _sattn(page_table, bias, qrep, knrep, vnrep, cache_k, cache_v, *, layer, pages):
    nb, n_pages = page_table.shape
    page = cache_k.shape[-1]
    units_per_seq = n_pages // pages
    seq_blk = lambda u, pt: (u // units_per_seq, 0, 0, 0)
    rep_spec = pl.BlockSpec((None, N_HEADS, HEAD_DIM, page), seq_blk)
    grid_spec = pltpu.PrefetchScalarGridSpec(
        num_scalar_prefetch=1,
        grid=(nb * units_per_seq,),
        in_specs=[pl.BlockSpec(memory_space=pltpu.SMEM), rep_spec, rep_spec, rep_spec,
                  pl.BlockSpec(memory_space=pl.ANY), pl.BlockSpec(memory_space=pl.ANY)],
        out_specs=pl.BlockSpec((None, GROUP_WIDTH, 1), lambda u, pt: (u // units_per_seq, 0, 0)),
        scratch_shapes=_paged_scratch(pages, page),
    )
    return pl.pallas_call(
        functools.partial(_sattn_kernel, n_units=nb * units_per_seq,
                          layer=layer, pages=pages, units_per_seq=units_per_seq),
        grid_spec=grid_spec,
        out_shape=jax.ShapeDtypeStruct((nb, GROUP_WIDTH, 1), F32),
        compiler_params=_cparams(1),
        name="paged_sb_attn",
    )(page_table, bias, qrep, knrep, vnrep, cache_k, cache_v)


def _outffn_paged_kernel(pt_ref, x_ref, a_ref, b_ref, c_ref, d_ref, wo_ref, g_ref, w1_ref, w2_ref, gf_ref,
                         bias_ref, qrep_ref, knrep_ref, vnrep_ref, kc_ref, vc_ref, y_ref, o_ref,
                         kbuf, vbuf, sems, carry_scr, acc_scr, *, final_norm, units_per_step, n_units, **cfg):
    i = pl.program_id(0)
    n_chunks = D_FF // FF_CHUNK

    @pl.when(i == 0)
    def _():
        _paged_prime(n_units, pt_ref, kc_ref, vc_ref, kbuf, vbuf, sems, **cfg)

    def before_chunk(c):
        for k in range(units_per_step):
            if (k * n_chunks) // units_per_step == c:
                _paged_run_unit(i * units_per_step + k, n_units, pt_ref, bias_ref, qrep_ref, knrep_ref, vnrep_ref,
                                kc_ref, vc_ref, o_ref, kbuf, vbuf, sems, carry_scr, acc_scr, chunk=k, **cfg)

    _outffn_kernel(x_ref, a_ref, b_ref, c_ref, d_ref, wo_ref, g_ref, w1_ref, w2_ref, gf_ref, y_ref,
                   final_norm=final_norm, before_chunk=before_chunk)


def _paged_fusable(n_steps, n_seqs, n_pages, pages):
    del n_pages, pages
    return n_seqs == n_steps


def _outffn_paged(x, a, b, c, d, wo, g, w1, w2, gf, page_table, bias, qrep, knrep, vnrep, cache_k, cache_v,
                  *, layer, tm, final_norm, pages):
    rows = x.shape[0]
    n_steps = rows // tm
    nb, n_pages = page_table.shape
    page = cache_k.shape[-1]
    units_per_seq = n_pages // pages
    units_per_step = nb * units_per_seq // n_steps
    row_blk = lambda i, pt: (i, 0)
    full = pl.BlockSpec((tm, D_MODEL), row_blk)
    gw = pl.BlockSpec((tm, GROUP_WIDTH), row_blk)
    seq_of = lambda i: (i * units_per_step) // units_per_seq
    rep_spec = pl.BlockSpec((None, N_HEADS, HEAD_DIM, page), lambda i, pt: (seq_of(i), 0, 0, 0))
    grid_spec = pltpu.PrefetchScalarGridSpec(
        num_scalar_prefetch=1,
        grid=(n_steps,),
        in_specs=[full, gw, gw, gw, gw, _layer_block(wo, layer), _layer_block(g, layer),
                  _layer_block(w1, layer), _layer_block(w2, layer), _whole(gf),
                  pl.BlockSpec(memory_space=pltpu.SMEM), rep_spec, rep_spec, rep_spec,
                  pl.BlockSpec(memory_space=pl.ANY), pl.BlockSpec(memory_space=pl.ANY)],
        out_specs=[full, pl.BlockSpec((None, GROUP_WIDTH, 1), lambda i, pt: (seq_of(i), 0, 0))],
        scratch_shapes=_paged_scratch(pages, page),
    )
    return pl.pallas_call(
        functools.partial(_outffn_paged_kernel, final_norm=final_norm, units_per_step=units_per_step,
                          n_units=nb * units_per_seq, layer=layer, pages=pages, units_per_seq=units_per_seq),
        grid_spec=grid_spec,
        out_shape=[jax.ShapeDtypeStruct((rows, D_MODEL), F32), jax.ShapeDtypeStruct((nb, GROUP_WIDTH, 1), F32)],
        compiler_params=_cparams(1),
        name="outffn_paged",
    )(page_table, x, a, b, c, d, wo, g, w1, w2, gf, bias, qrep, knrep, vnrep, cache_k, cache_v)


def _rope_tables(pos):
    inv = ROPE_BASE ** (-jnp.arange(0, HEAD_DIM, 2, dtype=F32) / HEAD_DIM)
    ang = pos.astype(F32)[:, None] * inv[None, :]
    cos, sin, zero = jnp.cos(ang), jnp.sin(ang), jnp.zeros_like(ang)
    tile = lambda a, b: jnp.tile(jnp.concatenate([a, b], axis=-1), (1, N_HEADS))
    return tile(cos, cos), tile(-sin, zero), tile(zero, sin)


def _retention_tables(c):
    lg = jnp.log1p(-jnp.exp2(-5.0 - jnp.arange(N_HEADS, dtype=F32)))
    idx = jnp.arange(c, dtype=F32)
    diff = idx[:, None] - idx[None, :]
    dmask = jnp.where(diff >= 0, jnp.exp(jnp.maximum(diff, 0.0) * lg[:, None, None]), 0.0)
    q_decay = jnp.exp((idx[None, :] + 1.0) * lg[:, None])
    k_decay = jnp.exp((c - 1.0 - idx[None, :]) * lg[:, None])
    chunk_decay = jnp.exp(c * lg)
    lanes = lambda hc: jnp.repeat(hc.T, HEAD_DIM, axis=1)
    return (jnp.transpose(dmask, (1, 0, 2)).reshape(c, N_HEADS * c), lanes(q_decay), lanes(k_decay),
            jnp.repeat(chunk_decay, HEAD_DIM)[None, :])


def kernel(x_prompt, x_sample, cache_k, cache_v, state_ret, state_conv, page_table, w_in, w_out, sgu_w, sgu_b, sgu_ln_g, sgu_ln_b, conv_w, conv_b, conv_ln_g, conv_ln_b, sb_bias, norm_mix, norm_ffn, norm_final, w_ff1, w_ff2):
    bp, lp, _ = x_prompt.shape
    bs, ls, _ = x_sample.shape
    assert ls == 1, "sample group is a single-token decode step"
    depth = w_in.shape[0]
    page_size = cache_k.shape[2]
    past_len = page_table.shape[1] * page_size
    rows_p = bp * lp
    tm_p = min(ROW_TILE, lp)
    tc = min(ROW_TILE, lp)

    w_in_b, w_out_b = w_in.astype(BF16), w_out.astype(BF16)
    w_ff1_b, w_ff2_b = w_ff1.astype(BF16), w_ff2.astype(BF16)
    cache_kt = jnp.transpose(cache_k, (0, 1, 3, 4, 2))
    cache_vt = jnp.transpose(cache_v, (0, 1, 3, 4, 2))

    rope_p = _rope_tables(jnp.arange(lp))
    rope_s = _rope_tables(jnp.full((bs,), past_len))
    shared_p = _retention_tables(CHUNK)
    _, qdec_s, _, _ = _retention_tables(1)
    gam_rows = jnp.tile(qdec_s.reshape(N_HEADS, HEAD_DIM)[:, :1], (bs, 1))

    rows3 = lambda a: a[:, None, :]
    norm_mix3, norm_ffn3 = rows3(norm_mix), rows3(norm_ffn)
    lng3, lnb3 = rows3(sgu_ln_g), rows3(sgu_ln_b)
    cb3, clg3, clb3 = rows3(conv_b), rows3(conv_ln_g), rows3(conv_ln_b)
    gf = norm_final[None, :]
    wsg = jnp.transpose(sgu_w, (0, 2, 1, 3)).reshape(depth, CHUNK, N_HEADS * CHUNK)
    bsg = jnp.repeat(jnp.transpose(sgu_b, (0, 2, 1)), HEAD_DIM, axis=2)
    w0 = rows3(jnp.repeat(sgu_w[:, :, 0, 0], HEAD_DIM, axis=1))
    b0 = rows3(jnp.repeat(sgu_b[:, :, 0], HEAD_DIM, axis=1))
    mix_consts = (wsg, bsg, conv_w, cb3, clg3, clb3)
    state_ret_rows = state_ret.reshape(depth, bs * N_HEADS, HEAD_DIM * HEAD_DIM)

    hp = x_prompt.reshape(rows_p, D_MODEL)
    hs = x_sample.reshape(bs, D_MODEL)
    kv_stack = jnp.zeros((depth, bp, N_HEADS, HEAD_DIM, lp), F32)
    kv_prompt = (kv_stack, kv_stack)
    outs = {n: [] for n in ("ks", "vs", "retp", "rets", "convp", "convs", "sguv")}

    n_pages = page_table.shape[1]
    pages = min(PAGES_PER_STEP, n_pages)
    ride_along = _paged_fusable(rows_p // tm_p, bs, n_pages, pages)
    per_head = lambda t: t.reshape(bs * N_HEADS, HEAD_DIM)
    lanes = lambda t: jnp.broadcast_to(t.reshape(bs, N_HEADS, HEAD_DIM, 1), (bs, N_HEADS, HEAD_DIM, page_size))

    for l in range(depth):
        last = l == depth - 1
        ua_s, vn_s, q_s, k_s, v_s, qr_s, kr_s, vc_s, sg_s, glu_s = _inproj_sample(
            hs, norm_mix3, w_in_b, rope_s, lng3, lnb3, layer=l)
        oa_s, od_s = _smix(ua_s, vn_s, glu_s, state_conv, w0, b0, conv_w, cb3, clg3, clb3, layer=l)
        q_h, k_h, v_h = per_head(qr_s), per_head(kr_s), per_head(vc_s)
        s_new, oc_s = _sret(state_ret_rows, jnp.repeat(q_h, HEAD_DIM, axis=1), jnp.repeat(k_h, HEAD_DIM, axis=1),
                            jnp.tile(v_h, (1, HEAD_DIM)), q_h, k_h, v_h, per_head(sg_s), gam_rows, layer=l)
        paged_args = (page_table, sb_bias[l], lanes(q_s), lanes(k_s), lanes(v_s), cache_kt, cache_vt)

        ua, vn, q, ktf, vtf, ktb, vtb, qr, kr, vc, sg, glu = _inproj_prompt(
            hp, norm_mix3, w_in_b, rope_p, lng3, lnb3, kv_prompt, layer=l, batch=bp, seq=lp, tm=tm_p)
        kv_prompt = (ktf, vtf)
        oa, oc, od, ret_p, conv_p = _mix(ua, vn, qr, kr, vc, sg, glu, mix_consts, shared_p,
                                             layer=l, batch=bp, seq=lp, tc=tc)
        ob = _attn(q, ktb, vtb, sb_bias[l], batch=bp, seq=lp)
        ffn_args = (hp, oa, ob, oc, od, w_out_b, norm_ffn3, w_ff1_b, w_ff2_b, gf)
        if ride_along:
            hp, ob_s = _outffn_paged(*ffn_args, *paged_args, layer=l, tm=tm_p, final_norm=last, pages=pages)
        else:
            hp = _outffn(*ffn_args, layer=l, tm=tm_p, final_norm=last)
            ob_s = _sattn(*paged_args, layer=l, pages=pages)
        outs["retp"].append(ret_p)
        outs["convp"].append(conv_p)

        hs = _outffn(hs, oa_s, ob_s.reshape(bs, GROUP_WIDTH), oc_s.reshape(bs, GROUP_WIDTH), od_s,
                     w_out_b, norm_ffn3, w_ff1_b, w_ff2_b, gf, layer=l, tm=bs, final_norm=last)
        outs["ks"].append(k_s.reshape(bs, 1, N_HEADS, HEAD_DIM))
        outs["vs"].append(v_s.reshape(bs, 1, N_HEADS, HEAD_DIM))
        outs["rets"].append(s_new.reshape(bs, N_HEADS, HEAD_DIM, HEAD_DIM))
        outs["convs"].append(glu_s[:, None, :])
        outs["sguv"].append(vn_s[:, None, :])

    st = lambda n: jnp.stack(outs[n])
    kp, vp = (jnp.transpose(t, (0, 1, 4, 2, 3)) for t in kv_prompt)
    conv_s = jnp.concatenate([state_conv[:, :, 1:, :], st("convs")], axis=2)
    return (hp.reshape(bp, lp, D_MODEL), hs.reshape(bs, 1, D_MODEL), kp, vp, st("ks"), st("vs"),
            st("retp"), st("rets"), st("convp"), conv_s, st("sguv"))
```
